```python
import functools
import jax
import jax.numpy as jnp
from jax import lax
import numpy as np

D_MODEL = 2048
BATCH = 2
SEQ = 4096
DEPTH = 2
DEC_BATCH = 8
DEC_SEQ = 4
PAST_LEN = 16384
PAGE_SIZE = 128

N_BRANCH = 4
BRANCH_W = D_MODEL // 4
A_W = BRANCH_W
A_GROUPS = 4
A_GW = A_W // A_GROUPS
CHUNK = 128
B_HEADS = 4
B_HD = BRANCH_W // B_HEADS
Q_BLOCK = 128
B_BIAS_INIT = -6.0
C_W = BRANCH_W
C_HD = 64
C_HEADS = C_W // C_HD
C_RW = 64
C_RA = 64
C_RG = 128
C_SHIFT = 3 * C_W + C_RW + C_RA + C_RG
D_W = BRANCH_W
CONV_K = 31
FFN_HIDDEN = -(-8 * D_MODEL // (3 * 256)) * 256

IN_SPLITS = [2 * A_W, 2 * A_W + 3 * BRANCH_W, 2 * A_W + 3 * BRANCH_W + C_SHIFT, 2 * A_W + 3 * BRANCH_W + C_SHIFT + 2 * D_W]
IN_W = 2 * A_W + 3 * BRANCH_W + C_SHIFT + 2 * D_W + N_BRANCH * D_MODEL
C_SPLITS = [C_W, 2 * C_W, 3 * C_W, 3 * C_W + C_RW, 3 * C_W + C_RW + C_RA]
RMS_EPS = 1e-6
LN_EPS = 1e-5
GN_EPS = 64e-5
F32 = jnp.float32

kernel_name = 'hybrid_gated_gmlp_stickbreak_rwkv7_conformer_step'


def rmsnorm(x, g):
    xf = x.astype(F32)
    y = xf * lax.rsqrt(jnp.mean(xf * xf, axis=-1, keepdims=True) + RMS_EPS)
    return (y * g.astype(F32)).astype(x.dtype)


def layernorm(x, g, b):
    xf = x.astype(F32)
    xc = xf - jnp.mean(xf, axis=-1, keepdims=True)
    var = jnp.mean(xc * xc, axis=-1, keepdims=True)
    return (xc * lax.rsqrt(var + LN_EPS) * g.astype(F32) + b.astype(F32)).astype(x.dtype)


def gmlp_branch(p_a, ln_g, ln_b, ws, bs, w_o):
    bsz, L, _ = p_a.shape
    z = jax.nn.gelu(p_a)
    u, v = jnp.split(z, 2, axis=-1)
    v = layernorm(v, ln_g, ln_b)
    c = min(L, CHUNK)
    causal = jnp.tril(jnp.ones((c, c), dtype=bool))
    w = jnp.where(causal, ws[:, :c, :c], jnp.zeros((), ws.dtype)).astype(v.dtype)
    vb = v.reshape(bsz, L // c, c, A_GROUPS, A_GW)
    s = jnp.einsum('gts,bnsgc->bntgc', w, vb) + bs[:, :c].T.astype(v.dtype)[None, None, :, :, None]
    y = (u * s.reshape(bsz, L, A_W)) @ w_o
    return y, v


def sb_attend(q, k, v, bias, q_pos, k_pos):
    z = jnp.einsum('bqhd,bkhd->bhqk', q, k).astype(F32) * (B_HD ** -0.5) + bias.astype(F32)[None, :, None, None]
    vis = k_pos[None, :] < q_pos[:, None]
    log_1m = jnp.where(vis, jax.nn.log_sigmoid(-z), 0.0)
    between = lax.cumsum(log_1m, axis=3, reverse=True) - log_1m
    att = jnp.where(vis, jnp.exp(jax.nn.log_sigmoid(z) + between), 0.0)
    return jnp.einsum('bhqk,bkhd->bqhd', att.astype(v.dtype), v)


def sb_prompt(q, k, v, bias):
    bsz, L, H, Dh = q.shape
    nb = L // Q_BLOCK
    qb = jnp.moveaxis(q.reshape(bsz, nb, Q_BLOCK, H, Dh), 1, 0)
    k_pos = jnp.arange(L)

    def one_block(args):
        q_blk, i = args
        return sb_attend(q_blk, k, v, bias, i * Q_BLOCK + jnp.arange(Q_BLOCK), k_pos)

    o = lax.map(one_block, (qb, jnp.arange(nb)))
    return jnp.moveaxis(o, 0, 1).reshape(bsz, L, H, Dh)


def sb_sample(q, k, v, bias, k_past=None, v_past=None):
    P = k_past.shape[1]
    L = q.shape[1]
    k_all = jnp.concatenate([k_past.astype(k.dtype), k], axis=1)
    v_all = jnp.concatenate([v_past.astype(v.dtype), v], axis=1)
    return sb_attend(q, k_all, v_all, bias, P + jnp.arange(L), jnp.arange(P + L))


def wkv_scan(s0, r, w, k, v, kk, a):
    seq = tuple(jnp.moveaxis(t, 1, 0) for t in (r, w, k, v, kk, a))

    def step(s, inp):
        r_t, w_t, k_t, v_t, kk_t, a_t = inp
        s_kk = jnp.einsum('bhvk,bhk->bhv', s, kk_t)
        s = s * w_t[:, :, None, :] - s_kk[..., None] * (kk_t * a_t)[:, :, None, :] + v_t[..., None] * k_t[:, :, None, :]
        return s, jnp.einsum('bhvk,bhk->bhv', s, r_t)

    s, o = lax.scan(step, s0.astype(F32), seq)
    return s, jnp.moveaxis(o, 0, 1)


def rwkv_branch(p_c, shift0, wkv0, mu, w0, w_up, a0, a_up, g_up, k_k, k_a, r_k, gn_g, gn_b, w_o):
    bsz, L, _ = p_c.shape
    pc = p_c.astype(F32)
    prev = jnp.concatenate([shift0.astype(F32)[:, None], pc[:, :-1]], axis=1)
    xs = pc + (prev - pc) * mu.astype(F32)
    r, k, v, xw, xa, xg = jnp.split(xs, C_SPLITS, axis=-1)
    w_log = -jax.nn.softplus(-(w0 + jnp.tanh(xw) @ w_up)) - 0.5
    decay = jnp.exp(-jnp.exp(w_log))
    a = jax.nn.sigmoid(a0 + xa @ a_up)
    g = jax.nn.sigmoid(xg) @ g_up
    hs = (bsz, L, C_HEADS, C_HD)
    kk = (k * k_k).reshape(hs)
    kk = kk * lax.rsqrt(jnp.maximum(jnp.sum(kk * kk, axis=-1, keepdims=True), 1e-24))
    k = k * (1.0 + (a - 1.0) * k_a)
    r, k, v, decay, a = (t.reshape(hs) for t in (r, k, v, decay, a))
    s, o = wkv_scan(wkv0, r, decay, k, v, kk, a)
    oc = o - jnp.mean(o, axis=-1, keepdims=True)
    o = oc * lax.rsqrt(jnp.mean(oc * oc, axis=-1, keepdims=True) + GN_EPS)
    o = o.reshape(bsz, L, C_W) * gn_g + gn_b
    o = o + (jnp.sum(r * k * r_k, axis=-1, keepdims=True) * v).reshape(bsz, L, C_W)
    y = (o * g) @ w_o
    return y.astype(p_c.dtype), s.astype(wkv0.dtype), p_c[:, -1]


def conv_branch(p_d, conv0, conv_w, conv_b, ln_g, ln_b, w_o):
    val, gate = jnp.split(p_d, 2, axis=-1)
    z = val * jax.nn.sigmoid(gate)
    zc = jnp.concatenate([conv0.astype(z.dtype), z], axis=1)
    y = lax.conv_general_dilated(zc, conv_w[:, None, :].astype(z.dtype), window_strides=(1,), padding='VALID',
                                 dimension_numbers=('NWC', 'WIO', 'NWC'), feature_group_count=D_W)
    y = jax.nn.silu(layernorm(y + conv_b.astype(z.dtype), ln_g, ln_b))
    return y @ w_o, zc[:, -(CONV_K - 1):]


def trunk_layer(x, lp, attend, shift0, wkv0, conv0):
    bsz, L, _ = x.shape
    h = rmsnorm(x, lp['norm_mix'])
    p = h @ lp['w_in']
    p_a, p_b, p_c, p_d, p_g = jnp.split(p, IN_SPLITS, axis=-1)
    y_a, v_a = gmlp_branch(p_a, lp['a_ln_g'], lp['a_ln_b'], lp['a_ws'], lp['a_bs'], lp['a_out'])
    q, k, v = jnp.split(p_b, 3, axis=-1)
    hs = (bsz, L, B_HEADS, B_HD)
    q = rmsnorm(q.reshape(hs), lp['b_qn'])
    k = rmsnorm(k.reshape(hs), lp['b_kn'])
    v = v.reshape(hs)
    y_b = attend(q, k, v, lp['b_bias']).reshape(bsz, L, BRANCH_W) @ lp['b_out']
    y_c, wkv1, shift1 = rwkv_branch(p_c, shift0, wkv0, lp['c_mu'], lp['c_w0'], lp['c_w_up'], lp['c_a0'], lp['c_a_up'],
                                    lp['c_g_up'], lp['c_k_k'], lp['c_k_a'], lp['c_r_k'], lp['c_gn_g'], lp['c_gn_b'], lp['c_out'])
    y_d, conv1 = conv_branch(p_d, conv0, lp['d_conv_w'], lp['d_conv_b'], lp['d_ln_g'], lp['d_ln_b'], lp['d_out'])
    gates = jax.nn.sigmoid(p_g.astype(F32)).reshape(bsz, L, N_BRANCH, D_MODEL)
    branches = jnp.stack([y_a, y_b, y_c, y_d], axis=2).astype(F32)
    merged = jnp.sum(gates * branches, axis=2).astype(x.dtype)
    x = x + merged @ lp['w_mix_out']
    h2 = rmsnorm(x, lp['norm_ffn'])
    x = x + (jax.nn.silu(h2 @ lp['f_gate']) * (h2 @ lp['f_up'])) @ lp['f_down']
    return x, k, v, wkv1, shift1, conv1, v_a


def setup_inputs(seed: int = 0) -> dict:
    key = jax.random.key(seed)
    ks = iter(jax.random.split(key, 48))
    nrm = lambda shape, scale: jax.random.normal(next(ks), shape, F32) * scale
    gain = lambda shape: 1.0 + 0.02 * jax.random.normal(next(ks), shape, F32)
    n_pages = PAST_LEN // PAGE_SIZE
    n_used = DEC_BATCH * n_pages
    n_phys = n_used + n_used // 4
    page_table = jax.random.permutation(next(ks), n_phys)[:n_used].reshape(DEC_BATCH, n_pages).astype(jnp.int32)
    return {
        'x_prompt': nrm((BATCH, SEQ, D_MODEL), 1.0),
        'x_sample': nrm((DEC_BATCH, DEC_SEQ, D_MODEL), 1.0),
        'cache_k': nrm((DEPTH, n_phys, PAGE_SIZE, B_HEADS, B_HD), 1.0),
        'cache_v': nrm((DEPTH, n_phys, PAGE_SIZE, B_HEADS, B_HD), 1.0),
        'page_table': page_table,
        'state_wkv': nrm((DEPTH, DEC_BATCH, C_HEADS, C_HD, C_HD), 0.3),
        'state_shift': nrm((DEPTH, DEC_BATCH, C_SHIFT), 1.0),
        'state_conv': nrm((DEPTH, DEC_BATCH, CONV_K - 1, D_W), 0.5),
        'norm_mix': gain((DEPTH, D_MODEL)),
        'w_in': nrm((DEPTH, D_MODEL, IN_W), D_MODEL ** -0.5),
        'a_ln_g': gain((DEPTH, A_W)),
        'a_ln_b': nrm((DEPTH, A_W), 0.02),
        'a_ws': nrm((DEPTH, A_GROUPS, CHUNK, CHUNK), CHUNK ** -0.5),
        'a_bs': 1.0 + nrm((DEPTH, A_GROUPS, CHUNK), 0.1),
        'a_out': nrm((DEPTH, A_W, D_MODEL), A_W ** -0.5),
        'b_qn': gain((DEPTH, B_HD)),
        'b_kn': gain((DEPTH, B_HD)),
        'b_bias': B_BIAS_INIT + nrm((DEPTH, B_HEADS), 0.1),
        'b_out': nrm((DEPTH, BRANCH_W, D_MODEL), BRANCH_W ** -0.5),
        'c_mu': jax.random.uniform(next(ks), (DEPTH, C_SHIFT), F32),
        'c_w0': nrm((DEPTH, C_W), 0.5),
        'c_w_up': nrm((DEPTH, C_RW, C_W), 0.5 * C_RW ** -0.5),
        'c_a0': nrm((DEPTH, C_W), 0.5),
        'c_a_up': nrm((DEPTH, C_RA, C_W), 0.5 * C_RA ** -0.5),
        'c_g_up': nrm((DEPTH, C_RG, C_W), C_RG ** -0.5),
        'c_k_k': 0.85 + nrm((DEPTH, C_W), 0.05),
        'c_k_a': 1.0 + nrm((DEPTH, C_W), 0.05),
        'c_r_k': nrm((DEPTH, C_HEADS, C_HD), 0.1),
        'c_gn_g': gain((DEPTH, C_W)),
        'c_gn_b': nrm((DEPTH, C_W), 0.02),
        'c_out': nrm((DEPTH, C_W, D_MODEL), C_W ** -0.5),
        'd_conv_w': nrm((DEPTH, CONV_K, D_W), CONV_K ** -0.5),
        'd_conv_b': nrm((DEPTH, D_W), 0.02),
        'd_ln_g': gain((DEPTH, D_W)),
        'd_ln_b': nrm((DEPTH, D_W), 0.02),
        'd_out': nrm((DEPTH, D_W, D_MODEL), D_W ** -0.5),
        'w_mix_out': nrm((DEPTH, D_MODEL, D_MODEL), D_MODEL ** -0.5),
        'norm_ffn': gain((DEPTH, D_MODEL)),
        'f_gate': nrm((DEPTH, D_MODEL, FFN_HIDDEN), D_MODEL ** -0.5),
        'f_up': nrm((DEPTH, D_MODEL, FFN_HIDDEN), D_MODEL ** -0.5),
        'f_down': nrm((DEPTH, FFN_HIDDEN, D_MODEL), FFN_HIDDEN ** -0.5),
    }


def reference(x_prompt, x_sample, cache_k, cache_v, page_table, state_wkv, state_shift, state_conv,
              norm_mix, w_in, a_ln_g, a_ln_b, a_ws, a_bs, a_out, b_qn, b_kn, b_bias, b_out,
              c_mu, c_w0, c_w_up, c_a0, c_a_up, c_g_up, c_k_k, c_k_a, c_r_k, c_gn_g, c_gn_b, c_out,
              d_conv_w, d_conv_b, d_ln_g, d_ln_b, d_out, w_mix_out, norm_ffn, f_gate, f_up, f_down):
    n_pages = page_table.shape[1]
    bp = x_prompt.shape[0]
    bs_ = x_sample.shape[0]
    dt = x_prompt.dtype
    yp, ys = x_prompt, x_sample
    kp_l, vp_l, ks_l, vs_l, wp_l, ws_l, sp_l, ss_l, cp_l, cs_l, gv_l = [], [], [], [], [], [], [], [], [], [], []
    for l in range(DEPTH):
        lp = {
            'norm_mix': norm_mix[l], 'w_in': w_in[l],
            'a_ln_g': a_ln_g[l], 'a_ln_b': a_ln_b[l], 'a_ws': a_ws[l], 'a_bs': a_bs[l], 'a_out': a_out[l],
            'b_qn': b_qn[l], 'b_kn': b_kn[l], 'b_bias': b_bias[l], 'b_out': b_out[l],
            'c_mu': c_mu[l], 'c_w0': c_w0[l], 'c_w_up': c_w_up[l], 'c_a0': c_a0[l], 'c_a_up': c_a_up[l],
            'c_g_up': c_g_up[l], 'c_k_k': c_k_k[l], 'c_k_a': c_k_a[l], 'c_r_k': c_r_k[l],
            'c_gn_g': c_gn_g[l], 'c_gn_b': c_gn_b[l], 'c_out': c_out[l],
            'd_conv_w': d_conv_w[l], 'd_conv_b': d_conv_b[l], 'd_ln_g': d_ln_g[l], 'd_ln_b': d_ln_b[l], 'd_out': d_out[l],
            'w_mix_out': w_mix_out[l], 'norm_ffn': norm_ffn[l],
            'f_gate': f_gate[l], 'f_up': f_up[l], 'f_down': f_down[l],
        }
        yp, kp, vp, wp, sp, cp, _ = trunk_layer(
            yp, lp, sb_prompt,
            jnp.zeros((bp, C_SHIFT), dt), jnp.zeros((bp, C_HEADS, C_HD, C_HD), F32), jnp.zeros((bp, CONV_K - 1, D_W), dt))
        k_past = cache_k[l][page_table].reshape(bs_, n_pages * PAGE_SIZE, B_HEADS, B_HD)
        v_past = cache_v[l][page_table].reshape(bs_, n_pages * PAGE_SIZE, B_HEADS, B_HD)
        attend_s = functools.partial(sb_sample, k_past=k_past, v_past=v_past)
        ys, ksn, vsn, wsn, ssn, csn, gvs = trunk_layer(ys, lp, attend_s, state_shift[l], state_wkv[l], state_conv[l])
        kp_l.append(kp); vp_l.append(vp); ks_l.append(ksn); vs_l.append(vsn)
        wp_l.append(wp); ws_l.append(wsn); sp_l.append(sp); ss_l.append(ssn)
        cp_l.append(cp); cs_l.append(csn); gv_l.append(gvs)
    k_prompt = jnp.stack(kp_l)
    v_prompt = jnp.stack(vp_l)
    k_sample = jnp.stack(ks_l)
    v_sample = jnp.stack(vs_l)
    wkv_prompt = jnp.stack(wp_l)
    wkv_sample = jnp.stack(ws_l)
    shift_prompt = jnp.stack(sp_l)
    shift_sample = jnp.stack(ss_l)
    conv_prompt = jnp.stack(cp_l)
    conv_sample = jnp.stack(cs_l)
    gmlp_v_sample = jnp.stack(gv_l)
    return (yp, ys, k_prompt, v_prompt, k_sample, v_sample, wkv_prompt, wkv_sample,
            shift_prompt, shift_sample, conv_prompt, conv_sample, gmlp_v_sample)
```

```python
import functools

import jax
import jax.numpy as jnp
from jax import lax
from jax.experimental import pallas as pl
from jax.experimental.pallas import tpu as pltpu

F32 = jnp.float32
BF16 = jnp.bfloat16

RMS_EPS = 1e-6
LN_EPS = 1e-5
GN_EPS = 64e-5
KK_EPS = 1e-24

LANES = 128
SUBLANES = 8
MXU_DIM = 256
MIB = 1024 * 1024

A_GROUPS = 4
ATT_HD = 128
RW_HD = 64
RW_LORA_W = 64
RW_LORA_A = 64
RW_LORA_G = 128
CONV_HALO = 32


def _cparams(semantics, vmem_mib):
    return pltpu.CompilerParams(dimension_semantics=semantics, vmem_limit_bytes=vmem_mib * MIB)


def _split_bf16(x):
    hi = x.astype(BF16)
    lo = (x - hi.astype(F32)).astype(BF16)
    return hi, lo


def _softplus(y):
    return jnp.maximum(y, 0.0) + jnp.log1p(jnp.exp(-jnp.abs(y)))


def _norm_matmul_kernel(x_ref, g_ref, w_ref, *rest, epilogue):
    if epilogue == "headnorm":
        hg_ref, o_ref, h_scr = rest
    else:
        o_ref, h_scr = rest

    @pl.when(pl.program_id(1) == 0)
    def _():
        x = x_ref[...]
        ms = jnp.mean(x * x, axis=-1, keepdims=True)
        h_scr[...] = (x * lax.rsqrt(ms + RMS_EPS) * g_ref[...]).astype(BF16)

    acc = jnp.dot(h_scr[...], w_ref[...], preferred_element_type=F32)
    if epilogue == "sigmoid":
        o_ref[...] = jax.nn.sigmoid(acc).astype(o_ref.dtype)
    elif epilogue == "headnorm":
        for c in range(acc.shape[1] // ATT_HD):
            sl = slice(c * ATT_HD, (c + 1) * ATT_HD)
            y = acc[:, sl]
            ms = jnp.mean(y * y, axis=-1, keepdims=True)
            o_ref[:, sl] = (y * lax.rsqrt(ms + RMS_EPS) * hg_ref[...]).astype(o_ref.dtype)
    else:
        o_ref[...] = acc.astype(o_ref.dtype)


def _norm_matmul(x, gain, w, *, tm, tn, epilogue=None, head_gain=None, out_dtype=F32):
    m, k = x.shape
    n = w.shape[1]
    in_specs = [
        pl.BlockSpec((tm, k), lambda i, j: (i, 0)),
        pl.BlockSpec((1, k), lambda i, j: (0, 0)),
        pl.BlockSpec((k, tn), lambda i, j: (0, j)),
    ]
    args = [x, gain.reshape(1, k), w]
    if epilogue == "headnorm":
        in_specs.append(pl.BlockSpec((1, ATT_HD), lambda i, j: (0, 0)))
        args.append(head_gain.reshape(1, ATT_HD))
    return pl.pallas_call(
        functools.partial(_norm_matmul_kernel, epilogue=epilogue),
        grid=(m // tm, n // tn),
        in_specs=in_specs,
        out_specs=pl.BlockSpec((tm, tn), lambda i, j: (i, j)),
        out_shape=jax.ShapeDtypeStruct((m, n), out_dtype),
        scratch_shapes=[pltpu.VMEM((tm, k), BF16)],
        compiler_params=_cparams(("parallel", "arbitrary"), 48),
    )(*args)


def _matmul_res_kernel(a_ref, w_ref, r_ref, o_ref):
    o_ref[...] = r_ref[...] + jnp.dot(a_ref[...], w_ref[...], preferred_element_type=F32)


def _matmul_residual(a, w, res, *, tm, tn):
    m, k = a.shape
    n = w.shape[1]
    return pl.pallas_call(
        _matmul_res_kernel,
        grid=(m // tm, n // tn),
        in_specs=[
            pl.BlockSpec((tm, k), lambda i, j: (i, 0)),
            pl.BlockSpec((k, tn), lambda i, j: (0, j)),
            pl.BlockSpec((tm, tn), lambda i, j: (i, j)),
        ],
        out_specs=pl.BlockSpec((tm, tn), lambda i, j: (i, j)),
        out_shape=jax.ShapeDtypeStruct((m, n), F32),
        compiler_params=_cparams(("parallel", "arbitrary"), 40),
    )(a, w, res)


def _ffn_kernel(x_ref, g_ref, wg_ref, wu_ref, wd_ref, o_ref, h_scr, acc_scr):
    j = pl.program_id(1)

    @pl.when(j == 0)
    def _():
        x = x_ref[...]
        ms = jnp.mean(x * x, axis=-1, keepdims=True)
        h_scr[...] = (x * lax.rsqrt(ms + RMS_EPS) * g_ref[...]).astype(BF16)
        acc_scr[...] = jnp.zeros_like(acc_scr)

    h = h_scr[...]
    gate = jnp.dot(h, wg_ref[...], preferred_element_type=F32)
    up = jnp.dot(h, wu_ref[...], preferred_element_type=F32)
    act = (gate * jax.nn.sigmoid(gate) * up).astype(BF16)
    acc_scr[...] += jnp.dot(act, wd_ref[...], preferred_element_type=F32)

    @pl.when(j == pl.num_programs(1) - 1)
    def _():
        o_ref[...] = x_ref[...] + acc_scr[...]


def _ffn(x, gain, wg, wu, wd, *, tm, th):
    m, d = x.shape
    hidden = wg.shape[1]
    return pl.pallas_call(
        _ffn_kernel,
        grid=(m // tm, hidden // th),
        in_specs=[
            pl.BlockSpec((tm, d), lambda i, j: (i, 0)),
            pl.BlockSpec((1, d), lambda i, j: (0, 0)),
            pl.BlockSpec((d, th), lambda i, j: (0, j)),
            pl.BlockSpec((d, th), lambda i, j: (0, j)),
            pl.BlockSpec((th, d), lambda i, j: (j, 0)),
        ],
        out_specs=pl.BlockSpec((tm, d), lambda i, j: (i, 0)),
        out_shape=jax.ShapeDtypeStruct((m, d), F32),
        scratch_shapes=[pltpu.VMEM((tm, d), BF16), pltpu.VMEM((tm, d), F32)],
        compiler_params=_cparams(("parallel", "arbitrary"), 52),
    )(x, gain.reshape(1, d), wg, wu, wd)


def _gmlp_kernel(p_ref, lng_ref, lnb_ref, wm_ref, bs_ref, o_ref, *v_out, chunk):
    z = jax.nn.gelu(p_ref[...])
    width = z.shape[1] // 2
    u = z[:, :width]
    v = z[:, width:]
    vc = v - jnp.mean(v, axis=-1, keepdims=True)
    var = jnp.mean(vc * vc, axis=-1, keepdims=True)
    v = vc * lax.rsqrt(var + LN_EPS) * lng_ref[...] + lnb_ref[...]
    if v_out:
        v_out[0][...] = v
    gw = width // A_GROUPS
    for ci in range(z.shape[0] // chunk):
        rows = slice(ci * chunk, (ci + 1) * chunk)
        for g in range(A_GROUPS):
            cols = slice(g * gw, (g + 1) * gw)
            s = jnp.dot(wm_ref[g], v[rows, cols].astype(BF16), preferred_element_type=F32) + bs_ref[g]
            o_ref[rows, cols] = (u[rows, cols] * s).astype(o_ref.dtype)


def _gmlp(p_a, ln_g, ln_b, w_mix, b_mix, *, tm, chunk, want_v):
    m, w2 = p_a.shape
    width = w2 // 2
    out_shape = [jax.ShapeDtypeStruct((m, width), BF16)]
    out_specs = [pl.BlockSpec((tm, width), lambda i: (i, 0))]
    if want_v:
        out_shape.append(jax.ShapeDtypeStruct((m, width), F32))
        out_specs.append(pl.BlockSpec((tm, width), lambda i: (i, 0)))
    res = pl.pallas_call(
        functools.partial(_gmlp_kernel, chunk=chunk),
        grid=(m // tm,),
        in_specs=[
            pl.BlockSpec((tm, w2), lambda i: (i, 0)),
            pl.BlockSpec((1, width), lambda i: (0, 0)),
            pl.BlockSpec((1, width), lambda i: (0, 0)),
            pl.BlockSpec(w_mix.shape, lambda i: (0, 0, 0)),
            pl.BlockSpec(b_mix.shape, lambda i: (0, 0, 0)),
        ],
        out_specs=out_specs,
        out_shape=out_shape,
        compiler_params=_cparams(("parallel",), 32),
    )(p_a, ln_g.reshape(1, width), ln_b.reshape(1, width), w_mix, b_mix)
    return res if want_v else (res[0], None)


def _stick_terms(z):
    t = jnp.log1p(jnp.exp(-jnp.abs(z)))
    return jnp.minimum(z, 0.0) - t, jnp.minimum(-z, 0.0) - t


def _suffix_sums(l1m, tri_ref):
    hi, lo = _split_bf16(l1m)
    tri = tri_ref[...]
    return (jnp.dot(hi, tri, preferred_element_type=F32) + jnp.dot(lo, tri, preferred_element_type=F32))


def _sb_prompt_kernel(bias_ref, q_ref, k_ref, v_ref, tri_ref, o_ref, *, tile, scale):
    h = pl.program_id(1)
    i = pl.program_id(2)
    bias = bias_ref[h]
    q = q_ref[0].astype(BF16)

    def tile_terms(j):
        start = pl.multiple_of(j * tile, tile)
        k_t = k_ref[0, pl.ds(start, tile), :].astype(BF16)
        v_t = v_ref[0, pl.ds(start, tile), :].astype(BF16)
        s = lax.dot_general(q, k_t, (((1,), (1,)), ((), ())), preferred_element_type=F32)
        z = s * scale + bias
        ls, l1m = _stick_terms(z)
        return ls, l1m, v_t

    ls, l1m, v_t = tile_terms(i)
    row = lax.broadcasted_iota(jnp.int32, (tile, tile), 0)
    col = lax.broadcasted_iota(jnp.int32, (tile, tile), 1)
    vis = col < row
    l1m = jnp.where(vis, l1m, 0.0)
    between = _suffix_sums(l1m, tri_ref)
    att = jnp.where(vis, jnp.exp(ls + between), 0.0)
    acc = jnp.dot(att.astype(BF16), v_t, preferred_element_type=F32)
    carry = jnp.sum(l1m, axis=1, keepdims=True)

    def body(jj, state):
        carry, acc = state
        ls, l1m, v_t = tile_terms(i - 1 - jj)
        between = _suffix_sums(l1m, tri_ref) + carry
        att = jnp.exp(ls + between)
        acc = acc + jnp.dot(att.astype(BF16), v_t, preferred_element_type=F32)
        carry = carry + jnp.sum(l1m, axis=1, keepdims=True)
        return carry, acc

    carry, acc = lax.fori_loop(0, i, body, (carry, acc))
    o_ref[0] = acc.astype(o_ref.dtype)


def _strict_upper(n):
    r = lax.broadcasted_iota(jnp.int32, (n, n), 0)
    c = lax.broadcasted_iota(jnp.int32, (n, n), 1)
    return (r > c).astype(BF16)


def _sb_prompt(q, k, v, bias, *, tile):
    b, l, w = q.shape
    heads = w // ATT_HD
    return pl.pallas_call(
        functools.partial(_sb_prompt_kernel, tile=tile, scale=ATT_HD ** -0.5),
        grid=(b, heads, l // tile),
        in_specs=[
            pl.BlockSpec(memory_space=pltpu.SMEM),
            pl.BlockSpec((1, tile, ATT_HD), lambda bi, h, i: (bi, i, h)),
            pl.BlockSpec((1, l, ATT_HD), lambda bi, h, i: (bi, 0, h)),
            pl.BlockSpec((1, l, ATT_HD), lambda bi, h, i: (bi, 0, h)),
            pl.BlockSpec((tile, tile), lambda bi, h, i: (0, 0)),
        ],
        out_specs=pl.BlockSpec((1, tile, ATT_HD), lambda bi, h, i: (bi, i, h)),
        out_shape=jax.ShapeDtypeStruct((b, l, w), BF16),
        compiler_params=_cparams(("parallel", "parallel", "arbitrary"), 40),
    )(bias, q, k, v, _strict_upper(tile))


def _sb_sample_kernel(pt_ref, bias_ref, qrows_ref, knew_ref, vnew_ref, *rest, pages_per_step, heads, n_q, scale):
    k_refs = rest[:pages_per_step]
    v_refs = rest[pages_per_step:2 * pages_per_step]
    tri_ref, o_ref, carry_scr, acc_scr = rest[2 * pages_per_step:]
    s = pl.program_id(1)
    page = tri_ref.shape[0]
    n_rows = heads * n_q
    q = qrows_ref[0].astype(BF16)

    def block(k_blk, v_blk, masked):
        k_b = k_blk.astype(BF16)
        sc = lax.dot_general(q, k_b, (((1,), (1,)), ((), ())), preferred_element_type=F32)
        z = sc * scale + bias_ref[...]
        ls, l1m = _stick_terms(z)
        if masked:
            row = lax.broadcasted_iota(jnp.int32, (n_rows, page), 0)
            col = lax.broadcasted_iota(jnp.int32, (n_rows, page), 1)
            vis = col < (row % n_q)
            l1m = jnp.where(vis, l1m, 0.0)
        between = _suffix_sums(l1m, tri_ref) + carry_scr[...]
        att = jnp.exp(ls + between)
        if masked:
            att = jnp.where(vis, att, 0.0)
        acc_scr[...] += jnp.dot(att.astype(BF16), v_blk.astype(BF16), preferred_element_type=F32)
        carry_scr[...] += jnp.sum(l1m, axis=1, keepdims=True)

    @pl.when(s == 0)
    def _():
        carry_scr[...] = jnp.zeros_like(carry_scr)
        acc_scr[...] = jnp.zeros_like(acc_scr)
        block(knew_ref[0], vnew_ref[0], True)

    for p in range(pages_per_step):
        block(k_refs[p][...], v_refs[p][...], False)

    @pl.when(s == pl.num_programs(1) - 1)
    def _():
        acc = acc_scr[...]
        for h in range(heads):
            o_ref[0, :, h * ATT_HD:(h + 1) * ATT_HD] = acc[h * n_q:(h + 1) * n_q, h * ATT_HD:(h + 1) * ATT_HD]


def _sb_sample(q, k_new, v_new, bias, cache_k, cache_v, page_table, layer, *, pages_per_step):
    b, n_q, w = q.shape
    heads = w // ATT_HD
    page = cache_k.shape[2]
    n_pages = page_table.shape[1]
    n_rows = heads * n_q
    eye = jnp.eye(heads, dtype=F32)
    qrows = jnp.einsum("bthd,hg->bhtgd", q.reshape(b, n_q, heads, ATT_HD), eye).reshape(b, n_rows, w)
    pad = ((0, 0), (0, page - n_q), (0, 0))
    k_pad = jnp.pad(k_new, pad)
    v_pad = jnp.pad(v_new, pad)
    bias_rows = jnp.broadcast_to(jnp.repeat(bias, n_q)[:, None], (n_rows, page))

    def page_spec(p):
        def imap(bi, s, pt):
            return (layer, pt[bi, n_pages - 1 - (s * pages_per_step + p)], 0, 0)
        return pl.BlockSpec((None, None, page, w), imap)

    grid_spec = pltpu.PrefetchScalarGridSpec(
        num_scalar_prefetch=1,
        grid=(b, n_pages // pages_per_step),
        in_specs=[
            pl.BlockSpec((n_rows, page), lambda bi, s, pt: (0, 0)),
            pl.BlockSpec((1, n_rows, w), lambda bi, s, pt: (bi, 0, 0)),
            pl.BlockSpec((1, page, w), lambda bi, s, pt: (bi, 0, 0)),
            pl.BlockSpec((1, page, w), lambda bi, s, pt: (bi, 0, 0)),
        ] + [page_spec(p) for p in range(pages_per_step)] * 2 + [
            pl.BlockSpec((page, page), lambda bi, s, pt: (0, 0)),
        ],
        out_specs=pl.BlockSpec((1, n_q, w), lambda bi, s, pt: (bi, 0, 0)),
        scratch_shapes=[pltpu.VMEM((n_rows, 1), F32), pltpu.VMEM((n_rows, w), F32)],
    )
    return pl.pallas_call(
        functools.partial(_sb_sample_kernel, pages_per_step=pages_per_step, heads=heads, n_q=n_q,
                          scale=ATT_HD ** -0.5),
        grid_spec=grid_spec,
        out_shape=jax.ShapeDtypeStruct((b, n_q, w), F32),
        compiler_params=_cparams(("parallel", "arbitrary"), 40),
    )(page_table, bias_rows, qrows, k_pad, v_pad,
      *([cache_k] * pages_per_step), *([cache_v] * pages_per_step), _strict_upper(page))


def _head_sums(x, bd_ref):
    hi, lo = _split_bf16(x)
    return _bd_dot(hi, bd_ref) + _bd_dot(lo, bd_ref)


def _bd_dot(lhs, bd_ref):
    bd = bd_ref[...]
    outs = [jnp.dot(lhs[:, c:c + MXU_DIM], bd, preferred_element_type=F32)
            for c in range(0, lhs.shape[1], MXU_DIM)]
    return jnp.concatenate(outs, axis=1)


def _rwkv_kernel(p_ref, shift0_ref, s0_ref, mu_ref, w0_ref, a0_ref, lora_ref, gup_ref, kk_ref, ka_ref, rk_ref,
                 gng_ref, gnb_ref, bd_ref, eye_ref,
                 o_ref, sout_ref,
                 st_scr, carry_scr, xs_scr, r_scr, w_scr, k_scr, v_scr, kn_scr, b_scr, g_scr, bonus_scr, oo_scr,
                 *, nb, tc, width):
    ci = pl.program_id(0)
    rows_per_b = RW_HD
    steps = min(tc, SUBLANES)

    @pl.when(ci == 0)
    def _():
        st_scr[...] = s0_ref[...]
        carry_scr[...] = shift0_ref[...]
        xs_scr[...] = jnp.zeros_like(xs_scr)

    for b in range(nb):
        pc = p_ref[b]
        xs_scr[SUBLANES:SUBLANES + tc, :] = pc
        xs_scr[SUBLANES - 1:SUBLANES, :] = carry_scr[b:b + 1, :]
        prev = xs_scr[SUBLANES - 1:SUBLANES - 1 + tc, :]
        carry_scr[b:b + 1, :] = pc[tc - 1:tc, :]
        xs = pc + (prev - pc) * mu_ref[...]
        r = xs[:, 0:width]
        k = xs[:, width:2 * width]
        v = xs[:, 2 * width:3 * width]
        xwa = xs[:, 3 * width:3 * width + RW_LORA_W + RW_LORA_A]
        xg = xs[:, 3 * width + RW_LORA_W + RW_LORA_A:]
        lane = lax.broadcasted_iota(jnp.int32, xwa.shape, 1)
        lora_in = jnp.where(lane < RW_LORA_W, jnp.tanh(xwa), xwa).astype(BF16)
        lora = jnp.dot(lora_in, lora_ref[...], preferred_element_type=F32)
        w_log = -_softplus(-(w0_ref[...] + lora[:, :width])) - 0.5
        decay = jnp.exp(-jnp.exp(w_log))
        a = jax.nn.sigmoid(a0_ref[...] + lora[:, width:])
        g = jnp.dot(jax.nn.sigmoid(xg).astype(BF16), gup_ref[...], preferred_element_type=F32)
        kk = k * kk_ref[...]
        kk = kk * lax.rsqrt(jnp.maximum(_head_sums(kk * kk, bd_ref), KK_EPS))
        k2 = k * (1.0 + (a - 1.0) * ka_ref[...])
        r_scr[b, 0:tc, :] = r
        w_scr[b, 0:tc, :] = decay
        k_scr[b, 0:tc, :] = k2
        v_scr[b, 0:tc, :] = v
        kn_scr[b, 0:tc, :] = kk
        b_scr[b, 0:tc, :] = kk * a
        g_scr[b, 0:tc, :] = g
        bonus_scr[b, 0:tc, :] = _head_sums(r * k2 * rk_ref[...], bd_ref) * v

    eye = eye_ref[...]
    eye_bf = eye.astype(BF16)

    def bcast(row):
        return jnp.broadcast_to(row, (rows_per_b, width))

    def group(gi, _):
        base = pl.multiple_of(gi * SUBLANES, SUBLANES)
        blk = {name: [scr[b, pl.ds(base, steps), :] for b in range(nb)]
               for name, scr in (("r", r_scr), ("w", w_scr), ("k", k_scr), ("v", v_scr), ("kn", kn_scr), ("b", b_scr))}
        state = [st_scr[b * rows_per_b:(b + 1) * rows_per_b, :] for b in range(nb)]
        out_rows = [[] for _ in range(nb)]
        for j in range(steps):
            row = lambda name, b: blk[name][b][j:j + 1, :]
            parts = []
            for b in range(nb):
                parts.extend(_split_bf16(state[b] * bcast(row("kn", b))))
            for b in range(nb):
                v_hi, v_lo = _split_bf16(row("v", b))
                parts.extend([eye_bf * bcast(v_hi), eye_bf * bcast(v_lo)])
            res = _bd_dot(jnp.concatenate(parts, axis=0), bd_ref)
            nr = rows_per_b
            outs = []
            for b in range(nb):
                s_kk = res[(2 * b) * nr:(2 * b + 1) * nr] + res[(2 * b + 1) * nr:(2 * b + 2) * nr]
                o0 = 2 * nb * nr
                v_col = res[o0 + (2 * b) * nr:o0 + (2 * b + 1) * nr] + res[o0 + (2 * b + 1) * nr:o0 + (2 * b + 2) * nr]
                state[b] = (state[b] * bcast(row("w", b)) - s_kk * bcast(row("b", b)) + v_col * bcast(row("k", b)))
                outs.extend(_split_bf16(state[b] * bcast(row("r", b))))
            res_o = _bd_dot(jnp.concatenate(outs, axis=0), bd_ref)
            for b in range(nb):
                o_col = res_o[(2 * b) * nr:(2 * b + 1) * nr] + res_o[(2 * b + 1) * nr:(2 * b + 2) * nr]
                out_rows[b].append(jnp.sum(o_col * eye, axis=0, keepdims=True))
        for b in range(nb):
            st_scr[b * rows_per_b:(b + 1) * rows_per_b, :] = state[b]
            oo_scr[b, pl.ds(base, steps), :] = jnp.concatenate(out_rows[b], axis=0)
        return 0

    lax.fori_loop(0, tc // steps, group, 0)

    inv_n = 1.0 / RW_HD
    for b in range(nb):
        o = oo_scr[b, 0:tc, :]
        oc = o - _head_sums(o, bd_ref) * inv_n
        var = _head_sums(oc * oc, bd_ref) * inv_n
        on = oc * lax.rsqrt(var + GN_EPS) * gng_ref[...] + gnb_ref[...] + bonus_scr[b, 0:tc, :]
        o_ref[b] = (on * g_scr[b, 0:tc, :]).astype(o_ref.dtype)

    @pl.when(ci == pl.num_programs(0) - 1)
    def _():
        sout_ref[...] = st_scr[...]


def _rwkv(p_c, shift0, wkv0, mu, w0, w_up, a0, a_up, g_up, k_k, k_a, r_k, gn_g, gn_b, *, tc):
    nb, l, cs = p_c.shape
    width = w0.shape[0]
    heads = width // RW_HD
    s0 = jnp.transpose(wkv0.astype(F32), (0, 2, 1, 3)).reshape(nb * RW_HD, width)
    lora = jnp.zeros((RW_LORA_W + RW_LORA_A, 2 * width), F32)
    lora = lora.at[:RW_LORA_W, :width].set(w_up).at[RW_LORA_W:, width:].set(a_up).astype(BF16)
    lane = jnp.arange(MXU_DIM)
    bd = (lane[:, None] // RW_HD == lane[None, :] // RW_HD).astype(BF16)
    eye = (jnp.arange(RW_HD)[:, None] == (jnp.arange(width)[None, :] % RW_HD)).astype(F32)
    row = lambda t: t.reshape(1, -1).astype(F32)
    full = lambda arr: pl.BlockSpec(arr.shape, lambda c: (0,) * arr.ndim)
    small = [shift0.astype(F32), s0, row(mu), row(w0), row(a0), lora, g_up.astype(BF16), row(k_k), row(k_a),
             row(r_k), row(gn_g), row(gn_b), bd, eye]
    chunk_scr = pltpu.VMEM((nb, -(-tc // SUBLANES) * SUBLANES, width), F32)
    out, s_out = pl.pallas_call(
        functools.partial(_rwkv_kernel, nb=nb, tc=tc, width=width),
        grid=(l // tc,),
        in_specs=[pl.BlockSpec((nb, tc, cs), lambda c: (0, c, 0))] + [full(t) for t in small],
        out_specs=[pl.BlockSpec((nb, tc, width), lambda c: (0, c, 0)),
                   pl.BlockSpec((nb * RW_HD, width), lambda c: (0, 0))],
        out_shape=[jax.ShapeDtypeStruct((nb, l, width), BF16),
                   jax.ShapeDtypeStruct((nb * RW_HD, width), F32)],
        scratch_shapes=[pltpu.VMEM((nb * RW_HD, width), F32), pltpu.VMEM((nb, cs), F32),
                        pltpu.VMEM((SUBLANES + tc, cs), F32)] + [chunk_scr] * 9,
        compiler_params=_cparams(("arbitrary",), 40),
    )(p_c, *small)
    s_new = jnp.transpose(s_out.reshape(nb, RW_HD, heads, RW_HD), (0, 2, 1, 3))
    return out, s_new


def _conv_kernel(p_ref, c0_ref, w_ref, cb_ref, lng_ref, lnb_ref, o_ref, c1_ref, zc_scr, *, tm, taps):
    i = pl.program_id(1)
    width = o_ref.shape[2]
    first = CONV_HALO - (taps - 1)

    @pl.when(i == 0)
    def _():
        zc_scr[0:first, :] = jnp.zeros((first, width), F32)
        zc_scr[first:CONV_HALO, :] = c0_ref[0]

    @pl.when(i > 0)
    def _():
        zc_scr[0:CONV_HALO, :] = zc_scr[tm:tm + CONV_HALO, :]

    p = p_ref[0]
    zc_scr[CONV_HALO:CONV_HALO + tm, :] = p[:, :width] * jax.nn.sigmoid(p[:, width:])
    y = jnp.zeros((tm, width), F32)
    for j in range(taps):
        y = y + zc_scr[first + j:first + j + tm, :] * w_ref[j:j + 1, :]
    y = y + cb_ref[...]
    yc = y - jnp.mean(y, axis=-1, keepdims=True)
    var = jnp.mean(yc * yc, axis=-1, keepdims=True)
    yn = yc * lax.rsqrt(var + LN_EPS) * lng_ref[...] + lnb_ref[...]
    o_ref[0] = (yn * jax.nn.sigmoid(yn)).astype(o_ref.dtype)

    @pl.when(i == pl.num_programs(1) - 1)
    def _():
        c1_ref[0] = zc_scr[tm + first:tm + CONV_HALO, :]


def _conv(p_d, conv0, conv_w, conv_b, ln_g, ln_b, *, tm):
    nb, l, w2 = p_d.shape
    width = w2 // 2
    taps = conv_w.shape[0]
    row = lambda t: t.reshape(1, width).astype(F32)
    return pl.pallas_call(
        functools.partial(_conv_kernel, tm=tm, taps=taps),
        grid=(nb, l // tm),
        in_specs=[
            pl.BlockSpec((1, tm, w2), lambda b, i: (b, i, 0)),
            pl.BlockSpec((1, taps - 1, width), lambda b, i: (b, 0, 0)),
            pl.BlockSpec((taps, width), lambda b, i: (0, 0)),
            pl.BlockSpec((1, width), lambda b, i: (0, 0)),
            pl.BlockSpec((1, width), lambda b, i: (0, 0)),
            pl.BlockSpec((1, width), lambda b, i: (0, 0)),
        ],
        out_specs=[pl.BlockSpec((1, tm, width), lambda b, i: (b, i, 0)),
                   pl.BlockSpec((1, taps - 1, width), lambda b, i: (b, 0, 0))],
        out_shape=[jax.ShapeDtypeStruct((nb, l, width), BF16),
                   jax.ShapeDtypeStruct((nb, taps - 1, width), F32)],
        scratch_shapes=[pltpu.VMEM((CONV_HALO + tm, width), F32)],
        compiler_params=_cparams(("parallel", "arbitrary"), 32),
    )(p_d, conv0.astype(F32), conv_w.astype(F32), row(conv_b), row(ln_g), row(ln_b))


def _merge_kernel(pa_ref, pb_ref, pc_ref, pd_ref, wa_ref, wb_ref, wc_ref, wd_ref,
                  ga_ref, gb_ref, gc_ref, gd_ref, o_ref):
    acc = None
    for p_ref, w_ref, g_ref in ((pa_ref, wa_ref, ga_ref), (pb_ref, wb_ref, gb_ref),
                                (pc_ref, wc_ref, gc_ref), (pd_ref, wd_ref, gd_ref)):
        y = g_ref[...].astype(F32) * jnp.dot(p_ref[...], w_ref[...], preferred_element_type=F32)
        acc = y if acc is None else acc + y
    o_ref[...] = acc.astype(o_ref.dtype)


def _merge(pres, outs, gates, *, tm, tn):
    m, width = pres[0].shape
    d = outs[0].shape[1]
    nj = d // tn
    gate_spec = lambda bidx: pl.BlockSpec((tm, tn), lambda i, j: (i, bidx * nj + j))
    return pl.pallas_call(
        _merge_kernel,
        grid=(m // tm, nj),
        in_specs=[pl.BlockSpec((tm, width), lambda i, j: (i, 0))] * 4
                 + [pl.BlockSpec((width, tn), lambda i, j: (0, j))] * 4
                 + [gate_spec(bidx) for bidx in range(4)],
        out_specs=pl.BlockSpec((tm, tn), lambda i, j: (i, j)),
        out_shape=jax.ShapeDtypeStruct((m, d), BF16),
        compiler_params=_cparams(("parallel", "arbitrary"), 32),
    )(*pres, *outs, gates, gates, gates, gates)


def _pick_tile(m, pref):
    return pref if m % pref == 0 else m


def _trunk_layer(x, lw, attend, shift0, wkv0, conv0, *, gmlp_chunk, want_v):
    nb, l, d = x.shape
    m = nb * l
    x2 = x.reshape(m, d)
    tm = _pick_tile(m, 1024)
    tm_s = _pick_tile(m, 512)
    nm = functools.partial(_norm_matmul, x2, lw["norm_mix"])
    p_a = nm(lw["w_a"], tm=tm, tn=512)
    q = nm(lw["w_q"], tm=tm, tn=512, epilogue="headnorm", head_gain=lw["b_qn"])
    k = nm(lw["w_k"], tm=tm, tn=512, epilogue="headnorm", head_gain=lw["b_kn"])
    v = nm(lw["w_v"], tm=tm, tn=512)
    p_c = nm(lw["w_c"], tm=tm, tn=256)
    p_d = nm(lw["w_d"], tm=tm, tn=512)
    gates = nm(lw["w_g"], tm=tm, tn=512, epilogue="sigmoid", out_dtype=BF16)

    width = q.shape[1]
    pre_a, v_a = _gmlp(p_a, lw["a_ln_g"], lw["a_ln_b"], lw["a_mix"], lw["a_bias"],
                       tm=_pick_tile(m, 256), chunk=gmlp_chunk, want_v=want_v)
    pre_b = attend(q.reshape(nb, l, width), k.reshape(nb, l, width), v.reshape(nb, l, width))
    pre_c, wkv1 = _rwkv(p_c.reshape(nb, l, -1), shift0, wkv0, lw["c_mu"], lw["c_w0"], lw["c_w_up"], lw["c_a0"],
                        lw["c_a_up"], lw["c_g_up"], lw["c_k_k"], lw["c_k_a"], lw["c_r_k"], lw["c_gn_g"],
                        lw["c_gn_b"], tc=min(l, 128))
    shift1 = p_c.reshape(nb, l, -1)[:, -1]
    pre_d, conv1 = _conv(p_d.reshape(nb, l, -1), conv0, lw["d_conv_w"], lw["d_conv_b"], lw["d_ln_g"], lw["d_ln_b"],
                         tm=min(l, 256))
    merged = _merge([pre_a, pre_b.reshape(m, width).astype(BF16), pre_c.reshape(m, width), pre_d.reshape(m, width)],
                    [lw["a_out"], lw["b_out"], lw["c_out"], lw["d_out"]], gates, tm=tm_s, tn=512)
    x2 = _matmul_residual(merged, lw["w_mix_out"], x2, tm=tm_s, tn=512)
    x2 = _ffn(x2, lw["norm_ffn"], lw["f_gate"], lw["f_up"], lw["f_down"], tm=tm_s, th=512)
    return x2.reshape(nb, l, d), k, v, wkv1, shift1, conv1, v_a


def kernel(x_prompt, x_sample, cache_k, cache_v, page_table, state_wkv, state_shift, state_conv, norm_mix, w_in, a_ln_g, a_ln_b, a_ws, a_bs, a_out, b_qn, b_kn, b_bias, b_out, c_mu, c_w0, c_w_up, c_a0, c_a_up, c_g_up, c_k_k, c_k_a, c_r_k, c_gn_g, c_gn_b, c_out, d_conv_w, d_conv_b, d_ln_g, d_ln_b, d_out, w_mix_out, norm_ffn, f_gate, f_up, f_down):
    depth = w_in.shape[0]
    bp, seq, d_model = x_prompt.shape
    bs, dec = x_sample.shape[:2]
    a_w = a_ln_g.shape[1]
    bw = b_out.shape[1]
    heads = b_bias.shape[1]
    c_shift = c_mu.shape[1]
    d_w = d_ln_g.shape[1]
    chunk = a_ws.shape[2]
    taps = d_conv_w.shape[1]
    gw = a_w // A_GROUPS
    n_phys, page = cache_k.shape[1:3]
    cache_k = cache_k.reshape(depth, n_phys, page, bw)
    cache_v = cache_v.reshape(depth, n_phys, page, bw)
    splits = [0, 2 * a_w, 2 * a_w + bw, 2 * a_w + 2 * bw, 2 * a_w + 3 * bw, 2 * a_w + 3 * bw + c_shift,
              2 * a_w + 3 * bw + c_shift + 2 * d_w, w_in.shape[2]]
    causal = jnp.tril(jnp.ones((chunk, chunk), dtype=bool))
    causal_s = jnp.tril(jnp.ones((dec, dec), dtype=bool))

    yp, ys = x_prompt, x_sample
    outs = [[] for _ in range(11)]
    for l in range(depth):
        seg = lambda i: w_in[l][:, splits[i]:splits[i + 1]].astype(BF16)
        lw = {
            "norm_mix": norm_mix[l], "w_a": seg(0), "w_q": seg(1), "w_k": seg(2), "w_v": seg(3), "w_c": seg(4),
            "w_d": seg(5), "w_g": seg(6),
            "a_ln_g": a_ln_g[l], "a_ln_b": a_ln_b[l], "a_out": a_out[l].astype(BF16),
            "b_qn": b_qn[l], "b_kn": b_kn[l], "b_out": b_out[l].astype(BF16),
            "c_mu": c_mu[l], "c_w0": c_w0[l], "c_w_up": c_w_up[l], "c_a0": c_a0[l], "c_a_up": c_a_up[l],
            "c_g_up": c_g_up[l], "c_k_k": c_k_k[l], "c_k_a": c_k_a[l], "c_r_k": c_r_k[l],
            "c_gn_g": c_gn_g[l], "c_gn_b": c_gn_b[l], "c_out": c_out[l].astype(BF16),
            "d_conv_w": d_conv_w[l], "d_conv_b": d_conv_b[l], "d_ln_g": d_ln_g[l], "d_ln_b": d_ln_b[l],
            "d_out": d_out[l].astype(BF16),
            "w_mix_out": w_mix_out[l].astype(BF16), "norm_ffn": norm_ffn[l],
            "f_gate": f_gate[l].astype(BF16), "f_up": f_up[l].astype(BF16), "f_down": f_down[l].astype(BF16),
        }
        lw_p = dict(lw)
        lw_p["a_mix"] = jnp.where(causal, a_ws[l], 0.0).astype(BF16)
        lw_p["a_bias"] = jnp.broadcast_to(a_bs[l][:, :, None], (A_GROUPS, chunk, gw)).astype(F32)
        attend_p = functools.partial(_sb_prompt, bias=b_bias[l], tile=256)
        yp, kp, vp, wp, sp, cp, _ = _trunk_layer(
            yp, lw_p, attend_p, jnp.zeros((bp, c_shift), F32), jnp.zeros((bp, c_w0.shape[1] // RW_HD, RW_HD, RW_HD), F32),
            jnp.zeros((bp, taps - 1, d_w), F32), gmlp_chunk=chunk, want_v=False)

        lw_s = dict(lw)
        w_small = jnp.where(causal_s, a_ws[l][:, :dec, :dec], 0.0)
        lw_s["a_mix"] = jnp.einsum("ab,gts->gatbs", jnp.eye(bs, dtype=F32), w_small).reshape(
            A_GROUPS, bs * dec, bs * dec).astype(BF16)
        lw_s["a_bias"] = jnp.broadcast_to(jnp.tile(a_bs[l][:, :dec], (1, bs))[:, :, None],
                                          (A_GROUPS, bs * dec, gw)).astype(F32)
        attend_s = lambda q, k, v: _sb_sample(q, k, v, b_bias[l], cache_k, cache_v, page_table, l, pages_per_step=8)
        ys, ksn, vsn, wsn, ssn, csn, gvs = _trunk_layer(
            ys, lw_s, attend_s, state_shift[l], state_wkv[l], state_conv[l], gmlp_chunk=bs * dec, want_v=True)

        layer_out = (kp.reshape(bp, seq, heads, ATT_HD), vp.reshape(bp, seq, heads, ATT_HD),
                     ksn.reshape(bs, dec, heads, ATT_HD), vsn.reshape(bs, dec, heads, ATT_HD),
                     wp, wsn, sp, ssn, cp, csn, gvs.reshape(bs, dec, a_w))
        for acc, val in zip(outs, layer_out):
            acc.append(val)
    return (yp, ys) + tuple(jnp.stack(o) for o in outs)
```

```python
import functools

import jax
import jax.numpy as jnp
from jax import lax
from jax.experimental import pallas as pl
from jax.experimental.pallas import tpu as pltpu

F32 = jnp.float32
BF16 = jnp.bfloat16

RMS_EPS = 1e-6
LN_EPS = 1e-5
GN_EPS = 64e-5
KK_EPS = 1e-24

LANES = 128
SUBLANES = 8
MXU_DIM = 256
MIB = 1024 * 1024

A_GROUPS = 4
ATT_HD = 128
RW_HD = 64
RW_LORA_W = 64
RW_LORA_A = 64
RW_LORA_G = 128
CONV_HALO = 32


def _cparams(semantics, vmem_mib):
    return pltpu.CompilerParams(dimension_semantics=semantics, vmem_limit_bytes=vmem_mib * MIB)


def _split_bf16(x):
    hi = x.astype(BF16)
    lo = (x - hi.astype(F32)).astype(BF16)
    return hi, lo


def _softplus(y):
    return jnp.maximum(y, 0.0) + jnp.log1p(jnp.exp(-jnp.abs(y)))


def _in_proj_kernel(x_ref, g_ref, w_ref, qg_ref, kg_ref, *rest, segments):
    out_refs = rest[:len(segments)]
    h_scr = rest[len(segments)]
    j = pl.program_id(1)

    @pl.when(j == 0)
    def _():
        x = x_ref[...]
        ms = jnp.mean(x * x, axis=-1, keepdims=True)
        h_scr[...] = (x * lax.rsqrt(ms + RMS_EPS) * g_ref[...]).astype(BF16)

    acc = jnp.dot(h_scr[...], w_ref[...], preferred_element_type=F32)
    for (kind, start, stop), o_ref in zip(segments, out_refs):

        @pl.when((j >= start) & (j < stop))
        def _(kind=kind, o_ref=o_ref):
            if kind == "sigmoid":
                o_ref[...] = jax.nn.sigmoid(acc).astype(o_ref.dtype)
            elif kind in ("qnorm", "knorm"):
                hg_ref = qg_ref if kind == "qnorm" else kg_ref
                for c in range(acc.shape[1] // ATT_HD):
                    sl = slice(c * ATT_HD, (c + 1) * ATT_HD)
                    y = acc[:, sl]
                    ms = jnp.mean(y * y, axis=-1, keepdims=True)
                    o_ref[:, sl] = y * lax.rsqrt(ms + RMS_EPS) * hg_ref[...]
            else:
                o_ref[...] = acc


def _in_proj(x, gain, w, q_gain, k_gain, seg_kinds, seg_widths, *, tm, tn):
    m, k = x.shape
    n = w.shape[1]
    segments, out_specs, out_shape = [], [], []
    start = 0
    for kind, width in zip(seg_kinds, seg_widths):
        tiles = width // tn
        assert tiles * tn == width
        segments.append((kind, start, start + tiles))
        out_specs.append(pl.BlockSpec((tm, tn), lambda i, j, s=start, t=tiles: (i, jnp.clip(j - s, 0, t - 1))))
        out_shape.append(jax.ShapeDtypeStruct((m, width), BF16 if kind == "sigmoid" else F32))
        start += tiles
    assert start * tn == n
    return pl.pallas_call(
        functools.partial(_in_proj_kernel, segments=tuple(segments)),
        grid=(m // tm, n // tn),
        in_specs=[
            pl.BlockSpec((tm, k), lambda i, j: (i, 0)),
            pl.BlockSpec((1, k), lambda i, j: (0, 0)),
            pl.BlockSpec((k, tn), lambda i, j: (0, j)),
            pl.BlockSpec((1, ATT_HD), lambda i, j: (0, 0)),
            pl.BlockSpec((1, ATT_HD), lambda i, j: (0, 0)),
        ],
        out_specs=out_specs,
        out_shape=out_shape,
        scratch_shapes=[pltpu.VMEM((tm, k), BF16)],
        compiler_params=_cparams(("parallel", "arbitrary"), 48),
    )(x, gain.reshape(1, k), w, q_gain.reshape(1, ATT_HD), k_gain.reshape(1, ATT_HD))


def _matmul_res_kernel(a_ref, w_ref, r_ref, o_ref):
    o_ref[...] = r_ref[...] + jnp.dot(a_ref[...], w_ref[...], preferred_element_type=F32)


def _matmul_residual(a, w, res, *, tm, tn):
    m, k = a.shape
    n = w.shape[1]
    return pl.pallas_call(
        _matmul_res_kernel,
        grid=(m // tm, n // tn),
        in_specs=[
            pl.BlockSpec((tm, k), lambda i, j: (i, 0)),
            pl.BlockSpec((k, tn), lambda i, j: (0, j)),
            pl.BlockSpec((tm, tn), lambda i, j: (i, j)),
        ],
        out_specs=pl.BlockSpec((tm, tn), lambda i, j: (i, j)),
        out_shape=jax.ShapeDtypeStruct((m, n), F32),
        compiler_params=_cparams(("parallel", "arbitrary"), 40),
    )(a, w, res)


def _ffn_kernel(x_ref, g_ref, wg_ref, wu_ref, wd_ref, o_ref, h_scr, acc_scr):
    j = pl.program_id(1)

    @pl.when(j == 0)
    def _():
        x = x_ref[...]
        ms = jnp.mean(x * x, axis=-1, keepdims=True)
        h_scr[...] = (x * lax.rsqrt(ms + RMS_EPS) * g_ref[...]).astype(BF16)
        acc_scr[...] = jnp.zeros_like(acc_scr)

    h = h_scr[...]
    gate = jnp.dot(h, wg_ref[...], preferred_element_type=F32)
    up = jnp.dot(h, wu_ref[...], preferred_element_type=F32)
    act = (gate * jax.nn.sigmoid(gate) * up).astype(BF16)
    acc_scr[...] += jnp.dot(act, wd_ref[...], preferred_element_type=F32)

    @pl.when(j == pl.num_programs(1) - 1)
    def _():
        o_ref[...] = x_ref[...] + acc_scr[...]


def _ffn(x, gain, wg, wu, wd, *, tm, th):
    m, d = x.shape
    hidden = wg.shape[1]
    return pl.pallas_call(
        _ffn_kernel,
        grid=(m // tm, hidden // th),
        in_specs=[
            pl.BlockSpec((tm, d), lambda i, j: (i, 0)),
            pl.BlockSpec((1, d), lambda i, j: (0, 0)),
            pl.BlockSpec((d, th), lambda i, j: (0, j)),
            pl.BlockSpec((d, th), lambda i, j: (0, j)),
            pl.BlockSpec((th, d), lambda i, j: (j, 0)),
        ],
        out_specs=pl.BlockSpec((tm, d), lambda i, j: (i, 0)),
        out_shape=jax.ShapeDtypeStruct((m, d), F32),
        scratch_shapes=[pltpu.VMEM((tm, d), BF16), pltpu.VMEM((tm, d), F32)],
        compiler_params=_cparams(("parallel", "arbitrary"), 52),
    )(x, gain.reshape(1, d), wg, wu, wd)


def _gmlp_kernel(p_ref, lng_ref, lnb_ref, wm_ref, bs_ref, o_ref, *v_out, chunk):
    z = jax.nn.gelu(p_ref[...])
    width = z.shape[1] // 2
    u = z[:, :width]
    v = z[:, width:]
    vc = v - jnp.mean(v, axis=-1, keepdims=True)
    var = jnp.mean(vc * vc, axis=-1, keepdims=True)
    v = vc * lax.rsqrt(var + LN_EPS) * lng_ref[...] + lnb_ref[...]
    if v_out:
        v_out[0][...] = v
    gw = width // A_GROUPS
    for ci in range(z.shape[0] // chunk):
        rows = slice(ci * chunk, (ci + 1) * chunk)
        for g in range(A_GROUPS):
            cols = slice(g * gw, (g + 1) * gw)
            s = jnp.dot(wm_ref[g], v[rows, cols].astype(BF16), preferred_element_type=F32) + bs_ref[g]
            o_ref[rows, cols] = (u[rows, cols] * s).astype(o_ref.dtype)


def _gmlp(p_a, ln_g, ln_b, w_mix, b_mix, *, tm, chunk, want_v):
    m, w2 = p_a.shape
    width = w2 // 2
    out_shape = [jax.ShapeDtypeStruct((m, width), BF16)]
    out_specs = [pl.BlockSpec((tm, width), lambda i: (i, 0))]
    if want_v:
        out_shape.append(jax.ShapeDtypeStruct((m, width), F32))
        out_specs.append(pl.BlockSpec((tm, width), lambda i: (i, 0)))
    res = pl.pallas_call(
        functools.partial(_gmlp_kernel, chunk=chunk),
        grid=(m // tm,),
        in_specs=[
            pl.BlockSpec((tm, w2), lambda i: (i, 0)),
            pl.BlockSpec((1, width), lambda i: (0, 0)),
            pl.BlockSpec((1, width), lambda i: (0, 0)),
            pl.BlockSpec(w_mix.shape, lambda i: (0, 0, 0)),
            pl.BlockSpec(b_mix.shape, lambda i: (0, 0, 0)),
        ],
        out_specs=out_specs,
        out_shape=out_shape,
        compiler_params=_cparams(("parallel",), 32),
    )(p_a, ln_g.reshape(1, width), ln_b.reshape(1, width), w_mix, b_mix)
    return res if want_v else (res[0], None)


def _stick_terms(z):
    t = jnp.log(1.0 + jnp.exp(-jnp.abs(z)))
    return jnp.minimum(z, 0.0) - t, jnp.minimum(-z, 0.0) - t


def _suffix_sums(l1m, tri_ref):
    hi, lo = _split_bf16(l1m)
    n = l1m.shape[0]
    both = jnp.dot(jnp.concatenate([hi, lo], axis=0), tri_ref[...], preferred_element_type=F32)
    return both[:n] + both[n:]


def _sb_prompt_kernel(bias_ref, q_ref, k_ref, v_ref, tri_ref, o_ref, *, tq, tk, heads, scale):
    i = pl.program_id(1)
    band = tq // tk
    cols = [slice(h * ATT_HD, (h + 1) * ATT_HD) for h in range(heads)]
    q = [q_ref[0, :, c].astype(BF16) for c in cols]

    def visit(j, h, carry, acc, offset=None):
        start = pl.multiple_of(j * tk, tk)
        k_t = k_ref[0, pl.ds(start, tk), cols[h]].astype(BF16)
        v_t = v_ref[0, pl.ds(start, tk), cols[h]].astype(BF16)
        s = lax.dot_general(q[h], k_t, (((1,), (1,)), ((), ())), preferred_element_type=F32)
        ls, l1m = _stick_terms(s * scale + bias_ref[h])
        if offset is not None:
            row = lax.broadcasted_iota(jnp.int32, (tq, tk), 0)
            col = lax.broadcasted_iota(jnp.int32, (tq, tk), 1)
            vis = col + offset < row
            l1m = jnp.where(vis, l1m, 0.0)
        between = _suffix_sums(l1m, tri_ref)
        if carry is not None:
            between = between + carry
        att = jnp.exp(ls + between)
        if offset is not None:
            att = jnp.where(vis, att, 0.0)
        total = jnp.sum(l1m, axis=1, keepdims=True)
        out = jnp.dot(att.astype(BF16), v_t, preferred_element_type=F32)
        return (total if carry is None else carry + total), (out if acc is None else acc + out)

    state = [None] * (2 * heads)
    for t in reversed(range(band)):
        for h in range(heads):
            state[2 * h], state[2 * h + 1] = visit(i * band + t, h, state[2 * h], state[2 * h + 1], offset=t * tk)

    def body(jj, state):
        new = []
        for h in range(heads):
            new.extend(visit(i * band - 1 - jj, h, state[2 * h], state[2 * h + 1]))
        return tuple(new)

    state = lax.fori_loop(0, i * band, body, tuple(state))
    for h in range(heads):
        o_ref[0, :, cols[h]] = state[2 * h + 1].astype(o_ref.dtype)


def _strict_upper(n):
    r = lax.broadcasted_iota(jnp.int32, (n, n), 0)
    c = lax.broadcasted_iota(jnp.int32, (n, n), 1)
    return (r > c).astype(BF16)


def _sb_prompt(q, k, v, bias, *, tq, tk):
    b, l, w = q.shape
    heads = w // ATT_HD
    assert tq % tk == 0 and l % tq == 0
    return pl.pallas_call(
        functools.partial(_sb_prompt_kernel, tq=tq, tk=tk, heads=heads, scale=ATT_HD ** -0.5),
        grid=(b, l // tq),
        in_specs=[
            pl.BlockSpec(memory_space=pltpu.SMEM),
            pl.BlockSpec((1, tq, w), lambda bi, i: (bi, i, 0)),
            pl.BlockSpec((1, l, w), lambda bi, i: (bi, 0, 0), pipeline_mode=pl.Buffered(1)),
            pl.BlockSpec((1, l, w), lambda bi, i: (bi, 0, 0), pipeline_mode=pl.Buffered(1)),
            pl.BlockSpec((tk, tk), lambda bi, i: (0, 0)),
        ],
        out_specs=pl.BlockSpec((1, tq, w), lambda bi, i: (bi, i, 0)),
        out_shape=jax.ShapeDtypeStruct((b, l, w), BF16),
        compiler_params=_cparams(("parallel", "arbitrary"), 48),
    )(bias, q, k, v, _strict_upper(tk))


def _sb_sample_kernel(pt_ref, bias_ref, qrows_ref, knew_ref, vnew_ref, *rest, pages_per_step, heads, n_q, scale):
    k_refs = rest[:pages_per_step]
    v_refs = rest[pages_per_step:2 * pages_per_step]
    tri_ref, o_ref, carry_scr, acc_scr = rest[2 * pages_per_step:]
    s = pl.program_id(1)
    page = tri_ref.shape[0]
    n_rows = heads * n_q
    q = qrows_ref[0].astype(BF16)

    def block(k_blk, v_blk, masked):
        k_b = k_blk.astype(BF16)
        sc = lax.dot_general(q, k_b, (((1,), (1,)), ((), ())), preferred_element_type=F32)
        z = sc * scale + bias_ref[...]
        ls, l1m = _stick_terms(z)
        if masked:
            row = lax.broadcasted_iota(jnp.int32, (n_rows, page), 0)
            col = lax.broadcasted_iota(jnp.int32, (n_rows, page), 1)
            vis = col < (row % n_q)
            l1m = jnp.where(vis, l1m, 0.0)
        between = _suffix_sums(l1m, tri_ref) + carry_scr[...]
        att = jnp.exp(ls + between)
        if masked:
            att = jnp.where(vis, att, 0.0)
        acc_scr[...] += jnp.dot(att.astype(BF16), v_blk.astype(BF16), preferred_element_type=F32)
        carry_scr[...] += jnp.sum(l1m, axis=1, keepdims=True)

    @pl.when(s == 0)
    def _():
        carry_scr[...] = jnp.zeros_like(carry_scr)
        acc_scr[...] = jnp.zeros_like(acc_scr)
        block(knew_ref[0], vnew_ref[0], True)

    def heads_to_lanes(ref):
        return jnp.concatenate([ref[pl.ds(h, page, stride=heads), :] for h in range(heads)], axis=1)

    for p in range(pages_per_step):
        block(heads_to_lanes(k_refs[p]), heads_to_lanes(v_refs[p]), False)

    @pl.when(s == pl.num_programs(1) - 1)
    def _():
        acc = acc_scr[...]
        for h in range(heads):
            o_ref[0, :, h * ATT_HD:(h + 1) * ATT_HD] = acc[h * n_q:(h + 1) * n_q, h * ATT_HD:(h + 1) * ATT_HD]


def _sb_sample(q, k_new, v_new, bias, cache_k, cache_v, page_table, layer, *, pages_per_step):
    b, n_q, w = q.shape
    heads = w // ATT_HD
    page = cache_k.shape[2] // heads
    n_pages = page_table.shape[1]
    n_rows = heads * n_q
    eye = jnp.eye(heads, dtype=F32)
    qrows = jnp.einsum("bthd,hg->bhtgd", q.reshape(b, n_q, heads, ATT_HD), eye).reshape(b, n_rows, w)
    pad = ((0, 0), (0, page - n_q), (0, 0))
    k_pad = jnp.pad(k_new, pad)
    v_pad = jnp.pad(v_new, pad)
    bias_rows = jnp.broadcast_to(jnp.repeat(bias, n_q)[:, None], (n_rows, page))

    def page_spec(p):
        def imap(bi, s, pt):
            return (layer, pt[bi, n_pages - 1 - (s * pages_per_step + p)], 0, 0)
        return pl.BlockSpec((None, None, page * heads, ATT_HD), imap)

    grid_spec = pltpu.PrefetchScalarGridSpec(
        num_scalar_prefetch=1,
        grid=(b, n_pages // pages_per_step),
        in_specs=[
            pl.BlockSpec((n_rows, page), lambda bi, s, pt: (0, 0)),
            pl.BlockSpec((1, n_rows, w), lambda bi, s, pt: (bi, 0, 0)),
            pl.BlockSpec((1, page, w), lambda bi, s, pt: (bi, 0, 0)),
            pl.BlockSpec((1, page, w), lambda bi, s, pt: (bi, 0, 0)),
        ] + [page_spec(p) for p in range(pages_per_step)] * 2 + [
            pl.BlockSpec((page, page), lambda bi, s, pt: (0, 0)),
        ],
        out_specs=pl.BlockSpec((1, n_q, w), lambda bi, s, pt: (bi, 0, 0)),
        scratch_shapes=[pltpu.VMEM((n_rows, 1), F32), pltpu.VMEM((n_rows, w), F32)],
    )
    return pl.pallas_call(
        functools.partial(_sb_sample_kernel, pages_per_step=pages_per_step, heads=heads, n_q=n_q,
                          scale=ATT_HD ** -0.5),
        grid_spec=grid_spec,
        out_shape=jax.ShapeDtypeStruct((b, n_q, w), F32),
        compiler_params=_cparams(("parallel", "arbitrary"), 40),
    )(page_table, bias_rows, qrows, k_pad, v_pad,
      *([cache_k] * pages_per_step), *([cache_v] * pages_per_step), _strict_upper(page))


def _head_sums(x, bd_ref):
    hi, lo = _split_bf16(x)
    return _bd_dot(hi, bd_ref) + _bd_dot(lo, bd_ref)


def _bd_dot(lhs, bd_ref):
    bd = bd_ref[...]
    outs = [jnp.dot(lhs[:, c:c + MXU_DIM], bd, preferred_element_type=F32)
            for c in range(0, lhs.shape[1], MXU_DIM)]
    return jnp.concatenate(outs, axis=1)


def _rwkv_kernel(p_ref, shift0_ref, s0_ref, mu_ref, w0_ref, a0_ref, lora_ref, gup_ref, kk_ref, ka_ref, rk_ref,
                 gng_ref, gnb_ref, bd_ref, eye_ref, eye2_ref,
                 o_ref, sout_ref,
                 st_scr, carry_scr, xs_scr, r_scr, w_scr, k_scr, v_scr, kn_scr, b_scr, g_scr, bonus_scr, oo_scr,
                 *, nb, tc, width):
    ci = pl.program_id(0)
    rows_per_b = RW_HD
    steps = min(tc, SUBLANES)

    @pl.when(ci == 0)
    def _():
        st_scr[...] = s0_ref[...]
        carry_scr[...] = shift0_ref[...]
        xs_scr[...] = jnp.zeros_like(xs_scr)
        if tc % SUBLANES:
            v_scr[...] = jnp.zeros_like(v_scr)

    for b in range(nb):
        pc = p_ref[b]
        xs_scr[SUBLANES:SUBLANES + tc, :] = pc
        xs_scr[SUBLANES - 1:SUBLANES, :] = carry_scr[b:b + 1, :]
        prev = xs_scr[SUBLANES - 1:SUBLANES - 1 + tc, :]
        carry_scr[b:b + 1, :] = pc[tc - 1:tc, :]
        xs = pc + (prev - pc) * mu_ref[...]
        r = xs[:, 0:width]
        k = xs[:, width:2 * width]
        v = xs[:, 2 * width:3 * width]
        xwa = xs[:, 3 * width:3 * width + RW_LORA_W + RW_LORA_A]
        xg = xs[:, 3 * width + RW_LORA_W + RW_LORA_A:]
        lane = lax.broadcasted_iota(jnp.int32, xwa.shape, 1)
        lora_in = jnp.where(lane < RW_LORA_W, jnp.tanh(xwa), xwa).astype(BF16)
        lora = jnp.dot(lora_in, lora_ref[...], preferred_element_type=F32)
        w_log = -_softplus(-(w0_ref[...] + lora[:, :width])) - 0.5
        decay = jnp.exp(-jnp.exp(w_log))
        a = jax.nn.sigmoid(a0_ref[...] + lora[:, width:])
        g = jnp.dot(jax.nn.sigmoid(xg).astype(BF16), gup_ref[...], preferred_element_type=F32)
        kk = k * kk_ref[...]
        kk = kk * lax.rsqrt(jnp.maximum(_head_sums(kk * kk, bd_ref), KK_EPS))
        k2 = k * (1.0 + (a - 1.0) * ka_ref[...])
        r_scr[b, 0:tc, :] = r
        w_scr[b, 0:tc, :] = decay
        k_scr[b, 0:tc, :] = k2
        v_scr[b, 0:tc, :] = v
        kn_scr[b, 0:tc, :] = kk
        b_scr[b, 0:tc, :] = kk * a
        g_scr[b, 0:tc, :] = g
        bonus_scr[b, 0:tc, :] = _head_sums(r * k2 * rk_ref[...], bd_ref) * v

    eye = eye_ref[...]
    eye_hi = eye.astype(BF16)
    eye_lo = eye2_ref[...]

    def bcast(row):
        return jnp.broadcast_to(row, (rows_per_b, width))

    def group(gi, _):
        base = pl.multiple_of(gi * SUBLANES, SUBLANES)
        blk = {name: [scr[b, pl.ds(base, steps), :] for b in range(nb)]
               for name, scr in (("r", r_scr), ("w", w_scr), ("k", k_scr), ("kn", kn_scr), ("b", b_scr))}
        state = [st_scr[b * rows_per_b:(b + 1) * rows_per_b, :] for b in range(nb)]
        out_rows = [[] for _ in range(nb)]
        nr = rows_per_b
        v_hi, v_lo = [], []
        for b in range(nb):
            v8 = v_scr[b, pl.ds(base, SUBLANES), :]
            hi = v8.astype(BF16).astype(F32)
            lo = v8 - hi
            lane = lax.broadcasted_iota(jnp.int32, lo.shape, 1)
            v_hi.append(hi)
            v_lo.append(jnp.where(lane % RW_HD < RW_HD // 2, pltpu.roll(lo, width - RW_HD // 2, 1),
                                  pltpu.roll(lo, RW_HD // 2, 1)))
        def seg_dot(parts):
            halves = [p[:, c:c + MXU_DIM] for p in parts for c in range(0, width, MXU_DIM)]
            res = jnp.dot(jnp.concatenate(halves, axis=0), bd_ref[...], preferred_element_type=F32)
            nh = width // MXU_DIM
            return [jnp.concatenate([res[(idx * nh + c) * nr:(idx * nh + c + 1) * nr] for c in range(nh)], axis=1)
                    for idx in range(len(parts))]

        def emit(b, o_col):
            out_rows[b].append(jnp.sum(o_col * eye, axis=0, keepdims=True))

        pending = [None] * nb
        for j in range(steps):
            row = lambda name, b: blk[name][b][j:j + 1, :]
            for b in range(nb):
                d = (eye_hi * bcast(v_hi[b][j:j + 1, :].astype(BF16))
                     + eye_lo * bcast(v_lo[b][j:j + 1, :].astype(BF16)))
                if pending[b] is None:
                    (v_col,) = seg_dot([d])
                else:
                    v_col, o_col = seg_dot([d, pending[b]])
                    emit(b, o_col)
                s_hi, s_lo = seg_dot(list(_split_bf16(state[b] * bcast(row("kn", b)))))
                state[b] = (state[b] * bcast(row("w", b)) - (s_hi + s_lo) * bcast(row("b", b))
                            + v_col * bcast(row("k", b)))
                pending[b] = (state[b] * bcast(row("r", b))).astype(BF16)
        for b in range(nb):
            emit(b, seg_dot([pending[b]])[0])
        for b in range(nb):
            st_scr[b * rows_per_b:(b + 1) * rows_per_b, :] = state[b]
            oo_scr[b, pl.ds(base, steps), :] = jnp.concatenate(out_rows[b], axis=0)
        return 0

    lax.fori_loop(0, tc // steps, group, 0)

    inv_n = 1.0 / RW_HD
    for b in range(nb):
        o = oo_scr[b, 0:tc, :]
        oc = o - _head_sums(o, bd_ref) * inv_n
        var = _head_sums(oc * oc, bd_ref) * inv_n
        on = oc * lax.rsqrt(var + GN_EPS) * gng_ref[...] + gnb_ref[...] + bonus_scr[b, 0:tc, :]
        o_ref[b] = (on * g_scr[b, 0:tc, :]).astype(o_ref.dtype)

    @pl.when(ci == pl.num_programs(0) - 1)
    def _():
        sout_ref[...] = st_scr[...]


def _rwkv(p_c, shift0, wkv0, mu, w0, w_up, a0, a_up, g_up, k_k, k_a, r_k, gn_g, gn_b, *, tc):
    nb, l, cs = p_c.shape
    width = w0.shape[0]
    heads = width // RW_HD
    s0 = jnp.transpose(wkv0.astype(F32), (0, 2, 1, 3)).reshape(nb * RW_HD, width)
    lora = jnp.zeros((RW_LORA_W + RW_LORA_A, 2 * width), F32)
    lora = lora.at[:RW_LORA_W, :width].set(w_up).at[RW_LORA_W:, width:].set(a_up).astype(BF16)
    lane = jnp.arange(MXU_DIM)
    bd = (lane[:, None] // RW_HD == lane[None, :] // RW_HD).astype(BF16)
    lane_k = jnp.arange(width)[None, :] % RW_HD
    eye = (jnp.arange(RW_HD)[:, None] == lane_k).astype(F32)
    eye2 = ((jnp.arange(RW_HD)[:, None] + RW_HD // 2) % RW_HD == lane_k).astype(BF16)
    row = lambda t: t.reshape(1, -1).astype(F32)
    full = lambda arr: pl.BlockSpec(arr.shape, lambda c: (0,) * arr.ndim)
    small = [shift0.astype(F32), s0, row(mu), row(w0), row(a0), lora, g_up.astype(BF16), row(k_k), row(k_a),
             row(r_k), row(gn_g), row(gn_b), bd, eye, eye2]
    chunk_scr = pltpu.VMEM((nb, -(-tc // SUBLANES) * SUBLANES, width), F32)
    out, s_out = pl.pallas_call(
        functools.partial(_rwkv_kernel, nb=nb, tc=tc, width=width),
        grid=(l // tc,),
        in_specs=[pl.BlockSpec((nb, tc, cs), lambda c: (0, c, 0))] + [full(t) for t in small],
        out_specs=[pl.BlockSpec((nb, tc, width), lambda c: (0, c, 0)),
                   pl.BlockSpec((nb * RW_HD, width), lambda c: (0, 0))],
        out_shape=[jax.ShapeDtypeStruct((nb, l, width), BF16),
                   jax.ShapeDtypeStruct((nb * RW_HD, width), F32)],
        scratch_shapes=[pltpu.VMEM((nb * RW_HD, width), F32), pltpu.VMEM((nb, cs), F32),
                        pltpu.VMEM((SUBLANES + tc, cs), F32)] + [chunk_scr] * 9,
        compiler_params=_cparams(("arbitrary",), 40),
    )(p_c, *small)
    s_new = jnp.transpose(s_out.reshape(nb, RW_HD, heads, RW_HD), (0, 2, 1, 3))
    return out, s_new


def _conv_kernel(p_ref, c0_ref, w_ref, cb_ref, lng_ref, lnb_ref, o_ref, c1_ref, zc_scr, *, tm, taps):
    i = pl.program_id(1)
    width = o_ref.shape[2]
    first = CONV_HALO - (taps - 1)

    @pl.when(i == 0)
    def _():
        zc_scr[0:first, :] = jnp.zeros((first, width), F32)
        zc_scr[first:CONV_HALO, :] = c0_ref[0]

    @pl.when(i > 0)
    def _():
        zc_scr[0:CONV_HALO, :] = zc_scr[tm:tm + CONV_HALO, :]

    p = p_ref[0]
    zc_scr[CONV_HALO:CONV_HALO + tm, :] = p[:, :width] * jax.nn.sigmoid(p[:, width:])
    y = jnp.zeros((tm, width), F32)
    for j in range(taps):
        y = y + zc_scr[first + j:first + j + tm, :] * w_ref[j:j + 1, :]
    y = y + cb_ref[...]
    yc = y - jnp.mean(y, axis=-1, keepdims=True)
    var = jnp.mean(yc * yc, axis=-1, keepdims=True)
    yn = yc * lax.rsqrt(var + LN_EPS) * lng_ref[...] + lnb_ref[...]
    o_ref[0] = (yn * jax.nn.sigmoid(yn)).astype(o_ref.dtype)

    @pl.when(i == pl.num_programs(1) - 1)
    def _():
        c1_ref[0] = zc_scr[tm + first:tm + CONV_HALO, :]


def _conv(p_d, conv0, conv_w, conv_b, ln_g, ln_b, *, tm):
    nb, l, w2 = p_d.shape
    width = w2 // 2
    taps = conv_w.shape[0]
    row = lambda t: t.reshape(1, width).astype(F32)
    return pl.pallas_call(
        functools.partial(_conv_kernel, tm=tm, taps=taps),
        grid=(nb, l // tm),
        in_specs=[
            pl.BlockSpec((1, tm, w2), lambda b, i: (b, i, 0)),
            pl.BlockSpec((1, taps - 1, width), lambda b, i: (b, 0, 0)),
            pl.BlockSpec((taps, width), lambda b, i: (0, 0)),
            pl.BlockSpec((1, width), lambda b, i: (0, 0)),
            pl.BlockSpec((1, width), lambda b, i: (0, 0)),
            pl.BlockSpec((1, width), lambda b, i: (0, 0)),
        ],
        out_specs=[pl.BlockSpec((1, tm, width), lambda b, i: (b, i, 0)),
                   pl.BlockSpec((1, taps - 1, width), lambda b, i: (b, 0, 0))],
        out_shape=[jax.ShapeDtypeStruct((nb, l, width), BF16),
                   jax.ShapeDtypeStruct((nb, taps - 1, width), F32)],
        scratch_shapes=[pltpu.VMEM((CONV_HALO + tm, width), F32)],
        compiler_params=_cparams(("parallel", "arbitrary"), 32),
    )(p_d, conv0.astype(F32), conv_w.astype(F32), row(conv_b), row(ln_g), row(ln_b))


def _merge_kernel(pa_ref, pb_ref, pc_ref, pd_ref, wa_ref, wb_ref, wc_ref, wd_ref,
                  ga_ref, gb_ref, gc_ref, gd_ref, o_ref):
    acc = None
    for p_ref, w_ref, g_ref in ((pa_ref, wa_ref, ga_ref), (pb_ref, wb_ref, gb_ref),
                                (pc_ref, wc_ref, gc_ref), (pd_ref, wd_ref, gd_ref)):
        y = g_ref[...].astype(F32) * jnp.dot(p_ref[...], w_ref[...], preferred_element_type=F32)
        acc = y if acc is None else acc + y
    o_ref[...] = acc.astype(o_ref.dtype)


def _merge(pres, outs, gates, *, tm, tn):
    m, width = pres[0].shape
    d = outs[0].shape[1]
    nj = d // tn
    gate_spec = lambda bidx: pl.BlockSpec((tm, tn), lambda i, j: (i, bidx * nj + j))
    return pl.pallas_call(
        _merge_kernel,
        grid=(m // tm, nj),
        in_specs=[pl.BlockSpec((tm, width), lambda i, j: (i, 0))] * 4
                 + [pl.BlockSpec((width, tn), lambda i, j: (0, j))] * 4
                 + [gate_spec(bidx) for bidx in range(4)],
        out_specs=pl.BlockSpec((tm, tn), lambda i, j: (i, j)),
        out_shape=jax.ShapeDtypeStruct((m, d), BF16),
        compiler_params=_cparams(("parallel", "arbitrary"), 32),
    )(*pres, *outs, gates, gates, gates, gates)


def _pick_tile(m, pref):
    return pref if m % pref == 0 else m


def _trunk_layer(x, lw, attend, shift0, wkv0, conv0, *, gmlp_chunk, want_v):
    nb, l, d = x.shape
    m = nb * l
    x2 = x.reshape(m, d)
    tm = _pick_tile(m, 1024)
    tm_s = _pick_tile(m, 512)
    p_a, q, k, v, p_c, p_d, gates = _in_proj(
        x2, lw["norm_mix"], lw["w_in"], lw["b_qn"], lw["b_kn"],
        ("plain", "qnorm", "knorm", "plain", "plain", "plain", "sigmoid"), lw["in_widths"], tm=tm, tn=MXU_DIM)

    width = q.shape[1]
    pre_a, v_a = _gmlp(p_a, lw["a_ln_g"], lw["a_ln_b"], lw["a_mix"], lw["a_bias"],
                       tm=_pick_tile(m, 256), chunk=gmlp_chunk, want_v=want_v)
    pre_b = attend(q.reshape(nb, l, width), k.reshape(nb, l, width), v.reshape(nb, l, width))
    pre_c, wkv1 = _rwkv(p_c.reshape(nb, l, -1), shift0, wkv0, lw["c_mu"], lw["c_w0"], lw["c_w_up"], lw["c_a0"],
                        lw["c_a_up"], lw["c_g_up"], lw["c_k_k"], lw["c_k_a"], lw["c_r_k"], lw["c_gn_g"],
                        lw["c_gn_b"], tc=min(l, 128))
    shift1 = p_c.reshape(nb, l, -1)[:, -1]
    pre_d, conv1 = _conv(p_d.reshape(nb, l, -1), conv0, lw["d_conv_w"], lw["d_conv_b"], lw["d_ln_g"], lw["d_ln_b"],
                         tm=min(l, 256))
    merged = _merge([pre_a, pre_b.reshape(m, width).astype(BF16), pre_c.reshape(m, width), pre_d.reshape(m, width)],
                    [lw["a_out"], lw["b_out"], lw["c_out"], lw["d_out"]], gates, tm=tm_s, tn=512)
    x2 = _matmul_residual(merged, lw["w_mix_out"], x2, tm=tm_s, tn=512)
    x2 = _ffn(x2, lw["norm_ffn"], lw["f_gate"], lw["f_up"], lw["f_down"], tm=tm_s, th=512)
    return x2.reshape(nb, l, d), k, v, wkv1, shift1, conv1, v_a


def kernel(x_prompt, x_sample, cache_k, cache_v, page_table, state_wkv, state_shift, state_conv, norm_mix, w_in, a_ln_g, a_ln_b, a_ws, a_bs, a_out, b_qn, b_kn, b_bias, b_out, c_mu, c_w0, c_w_up, c_a0, c_a_up, c_g_up, c_k_k, c_k_a, c_r_k, c_gn_g, c_gn_b, c_out, d_conv_w, d_conv_b, d_ln_g, d_ln_b, d_out, w_mix_out, norm_ffn, f_gate, f_up, f_down):
    depth = w_in.shape[0]
    bp, seq, d_model = x_prompt.shape
    bs, dec = x_sample.shape[:2]
    a_w = a_ln_g.shape[1]
    bw = b_out.shape[1]
    heads = b_bias.shape[1]
    c_shift = c_mu.shape[1]
    d_w = d_ln_g.shape[1]
    chunk = a_ws.shape[2]
    taps = d_conv_w.shape[1]
    gw = a_w // A_GROUPS
    n_phys, page = cache_k.shape[1:3]
    cache_k = cache_k.reshape(depth, n_phys, page * heads, ATT_HD)
    cache_v = cache_v.reshape(depth, n_phys, page * heads, ATT_HD)
    in_widths = (2 * a_w, bw, bw, bw, c_shift, 2 * d_w, 4 * d_model)
    causal = jnp.tril(jnp.ones((chunk, chunk), dtype=bool))
    causal_s = jnp.tril(jnp.ones((dec, dec), dtype=bool))

    yp, ys = x_prompt, x_sample
    outs = [[] for _ in range(11)]
    for l in range(depth):
        lw = {
            "norm_mix": norm_mix[l], "w_in": w_in[l].astype(BF16), "in_widths": in_widths,
            "a_ln_g": a_ln_g[l], "a_ln_b": a_ln_b[l], "a_out": a_out[l].astype(BF16),
            "b_qn": b_qn[l], "b_kn": b_kn[l], "b_out": b_out[l].astype(BF16),
            "c_mu": c_mu[l], "c_w0": c_w0[l], "c_w_up": c_w_up[l], "c_a0": c_a0[l], "c_a_up": c_a_up[l],
            "c_g_up": c_g_up[l], "c_k_k": c_k_k[l], "c_k_a": c_k_a[l], "c_r_k": c_r_k[l],
            "c_gn_g": c_gn_g[l], "c_gn_b": c_gn_b[l], "c_out": c_out[l].astype(BF16),
            "d_conv_w": d_conv_w[l], "d_conv_b": d_conv_b[l], "d_ln_g": d_ln_g[l], "d_ln_b": d_ln_b[l],
            "d_out": d_out[l].astype(BF16),
            "w_mix_out": w_mix_out[l].astype(BF16), "norm_ffn": norm_ffn[l],
            "f_gate": f_gate[l].astype(BF16), "f_up": f_up[l].astype(BF16), "f_down": f_down[l].astype(BF16),
        }
        lw_p = dict(lw)
        lw_p["a_mix"] = jnp.where(causal, a_ws[l], 0.0).astype(BF16)
        lw_p["a_bias"] = jnp.broadcast_to(a_bs[l][:, :, None], (A_GROUPS, chunk, gw)).astype(F32)
        attend_p = functools.partial(_sb_prompt, bias=b_bias[l], tq=1024, tk=256)
        yp, kp, vp, wp, sp, cp, _ = _trunk_layer(
            yp, lw_p, attend_p, jnp.zeros((bp, c_shift), F32), jnp.zeros((bp, c_w0.shape[1] // RW_HD, RW_HD, RW_HD), F32),
            jnp.zeros((bp, taps - 1, d_w), F32), gmlp_chunk=chunk, want_v=False)

        lw_s = dict(lw)
        w_small = jnp.where(causal_s, a_ws[l][:, :dec, :dec], 0.0)
        lw_s["a_mix"] = jnp.einsum("ab,gts->gatbs", jnp.eye(bs, dtype=F32), w_small).reshape(
            A_GROUPS, bs * dec, bs * dec).astype(BF16)
        lw_s["a_bias"] = jnp.broadcast_to(jnp.tile(a_bs[l][:, :dec], (1, bs))[:, :, None],
                                          (A_GROUPS, bs * dec, gw)).astype(F32)
        attend_s = lambda q, k, v: _sb_sample(q, k, v, b_bias[l], cache_k, cache_v, page_table, l, pages_per_step=8)
        ys, ksn, vsn, wsn, ssn, csn, gvs = _trunk_layer(
            ys, lw_s, attend_s, state_shift[l], state_wkv[l], state_conv[l], gmlp_chunk=bs * dec, want_v=True)

        layer_out = (kp.reshape(bp, seq, heads, ATT_HD), vp.reshape(bp, seq, heads, ATT_HD),
                     ksn.reshape(bs, dec, heads, ATT_HD), vsn.reshape(bs, dec, heads, ATT_HD),
                     wp, wsn, sp, ssn, cp, csn, gvs.reshape(bs, dec, a_w))
        for acc, val in zip(outs, layer_out):
            acc.append(val)
    return (yp, ys) + tuple(jnp.stack(o) for o in outs)
```

```python
import functools

import jax
import jax.numpy as jnp
from jax import lax
from jax.experimental import pallas as pl
from jax.experimental.pallas import tpu as pltpu

F32 = jnp.float32
BF16 = jnp.bfloat16

RMS_EPS = 1e-6
LN_EPS = 1e-5
GN_EPS = 64e-5
KK_EPS = 1e-24

LANES = 128
SUBLANES = 8
MXU_DIM = 256
MIB = 1024 * 1024

A_GROUPS = 4
ATT_HD = 128
RW_HD = 64
RW_LORA_W = 64
RW_LORA_A = 64
RW_LORA_G = 128
CONV_HALO = 32


def _cparams(semantics, vmem_mib):
    return pltpu.CompilerParams(dimension_semantics=semantics, vmem_limit_bytes=vmem_mib * MIB)


def _split_bf16(x):
    hi = x.astype(BF16)
    lo = (x - hi.astype(F32)).astype(BF16)
    return hi, lo


def _softplus(y):
    return jnp.maximum(y, 0.0) + jnp.log1p(jnp.exp(-jnp.abs(y)))


def _in_proj_kernel(x_ref, g_ref, w_ref, qg_ref, kg_ref, *rest, segments):
    out_refs = rest[:len(segments)]
    h_scr = rest[len(segments)]
    j = pl.program_id(1)

    @pl.when(j == 0)
    def _():
        x = x_ref[...]
        ms = jnp.mean(x * x, axis=-1, keepdims=True)
        h_scr[...] = (x * lax.rsqrt(ms + RMS_EPS) * g_ref[...]).astype(BF16)

    acc = jnp.dot(h_scr[...], w_ref[...], preferred_element_type=F32)
    for (kind, start, stop), o_ref in zip(segments, out_refs):
        in_segment = pl.when((j >= start) & (j < stop)) if len(segments) > 1 else (lambda f: f())

        @in_segment
        def _(kind=kind, o_ref=o_ref):
            if kind == "sigmoid":
                o_ref[...] = jax.nn.sigmoid(acc).astype(o_ref.dtype)
            elif kind in ("qnorm", "knorm"):
                hg_ref = qg_ref if kind == "qnorm" else kg_ref
                for c in range(acc.shape[1] // ATT_HD):
                    sl = slice(c * ATT_HD, (c + 1) * ATT_HD)
                    y = acc[:, sl]
                    ms = jnp.mean(y * y, axis=-1, keepdims=True)
                    o_ref[:, sl] = y * lax.rsqrt(ms + RMS_EPS) * hg_ref[...]
            else:
                o_ref[...] = acc


def _in_proj(x, gain, w, q_gain, k_gain, seg_kinds, seg_widths, *, tm, tn):
    m, k = x.shape
    n = w.shape[1]
    segments, out_specs, out_shape = [], [], []
    start = 0
    for kind, width in zip(seg_kinds, seg_widths):
        tiles = width // tn
        assert tiles * tn == width
        segments.append((kind, start, start + tiles))
        out_specs.append(pl.BlockSpec((tm, tn), lambda i, j, s=start, t=tiles: (i, jnp.clip(j - s, 0, t - 1))))
        out_shape.append(jax.ShapeDtypeStruct((m, width), BF16 if kind == "sigmoid" else F32))
        start += tiles
    assert start * tn == n
    return pl.pallas_call(
        functools.partial(_in_proj_kernel, segments=tuple(segments)),
        grid=(m // tm, n // tn),
        in_specs=[
            pl.BlockSpec((tm, k), lambda i, j: (i, 0)),
            pl.BlockSpec((1, k), lambda i, j: (0, 0)),
            pl.BlockSpec((k, tn), lambda i, j: (0, j)),
            pl.BlockSpec((1, ATT_HD), lambda i, j: (0, 0)),
            pl.BlockSpec((1, ATT_HD), lambda i, j: (0, 0)),
        ],
        out_specs=out_specs,
        out_shape=out_shape,
        scratch_shapes=[pltpu.VMEM((tm, k), BF16)],
        compiler_params=_cparams(("parallel", "arbitrary"), 48),
    )(x, gain.reshape(1, k), w, q_gain.reshape(1, ATT_HD), k_gain.reshape(1, ATT_HD))


def _matmul_res_kernel(a_ref, w_ref, r_ref, o_ref):
    o_ref[...] = r_ref[...] + jnp.dot(a_ref[...], w_ref[...], preferred_element_type=F32)


def _matmul_residual(a, w, res, *, tm, tn):
    m, k = a.shape
    n = w.shape[1]
    return pl.pallas_call(
        _matmul_res_kernel,
        grid=(m // tm, n // tn),
        in_specs=[
            pl.BlockSpec((tm, k), lambda i, j: (i, 0)),
            pl.BlockSpec((k, tn), lambda i, j: (0, j)),
            pl.BlockSpec((tm, tn), lambda i, j: (i, j)),
        ],
        out_specs=pl.BlockSpec((tm, tn), lambda i, j: (i, j)),
        out_shape=jax.ShapeDtypeStruct((m, n), F32),
        compiler_params=_cparams(("parallel", "arbitrary"), 40),
    )(a, w, res)


def _ffn_kernel(x_ref, g_ref, wg_ref, wu_ref, wd_ref, o_ref, h_scr, acc_scr):
    j = pl.program_id(1)

    @pl.when(j == 0)
    def _():
        x = x_ref[...]
        ms = jnp.mean(x * x, axis=-1, keepdims=True)
        h_scr[...] = (x * lax.rsqrt(ms + RMS_EPS) * g_ref[...]).astype(BF16)
        acc_scr[...] = jnp.zeros_like(acc_scr)

    h = h_scr[...]
    gate = jnp.dot(h, wg_ref[...], preferred_element_type=F32)
    up = jnp.dot(h, wu_ref[...], preferred_element_type=F32)
    act = (gate * jax.nn.sigmoid(gate) * up).astype(BF16)
    acc_scr[...] += jnp.dot(act, wd_ref[...], preferred_element_type=F32)

    @pl.when(j == pl.num_programs(1) - 1)
    def _():
        o_ref[...] = x_ref[...] + acc_scr[...]


def _ffn(x, gain, wg, wu, wd, *, tm, th):
    m, d = x.shape
    hidden = wg.shape[1]
    return pl.pallas_call(
        _ffn_kernel,
        grid=(m // tm, hidden // th),
        in_specs=[
            pl.BlockSpec((tm, d), lambda i, j: (i, 0)),
            pl.BlockSpec((1, d), lambda i, j: (0, 0)),
            pl.BlockSpec((d, th), lambda i, j: (0, j)),
            pl.BlockSpec((d, th), lambda i, j: (0, j)),
            pl.BlockSpec((th, d), lambda i, j: (j, 0)),
        ],
        out_specs=pl.BlockSpec((tm, d), lambda i, j: (i, 0)),
        out_shape=jax.ShapeDtypeStruct((m, d), F32),
        scratch_shapes=[pltpu.VMEM((tm, d), BF16), pltpu.VMEM((tm, d), F32)],
        compiler_params=_cparams(("parallel", "arbitrary"), 52),
    )(x, gain.reshape(1, d), wg, wu, wd)


def _gmlp_kernel(p_ref, lng_ref, lnb_ref, wm_ref, bs_ref, o_ref, *v_out, chunk):
    z = jax.nn.gelu(p_ref[...])
    width = z.shape[1] // 2
    u = z[:, :width]
    v = z[:, width:]
    vc = v - jnp.mean(v, axis=-1, keepdims=True)
    var = jnp.mean(vc * vc, axis=-1, keepdims=True)
    v = vc * lax.rsqrt(var + LN_EPS) * lng_ref[...] + lnb_ref[...]
    if v_out:
        v_out[0][...] = v
    gw = width // A_GROUPS
    for ci in range(z.shape[0] // chunk):
        rows = slice(ci * chunk, (ci + 1) * chunk)
        for g in range(A_GROUPS):
            cols = slice(g * gw, (g + 1) * gw)
            s = jnp.dot(wm_ref[g], v[rows, cols].astype(BF16), preferred_element_type=F32) + bs_ref[g]
            o_ref[rows, cols] = (u[rows, cols] * s).astype(o_ref.dtype)


def _gmlp(p_a, ln_g, ln_b, w_mix, b_mix, *, tm, chunk, want_v):
    m, w2 = p_a.shape
    width = w2 // 2
    out_shape = [jax.ShapeDtypeStruct((m, width), BF16)]
    out_specs = [pl.BlockSpec((tm, width), lambda i: (i, 0))]
    if want_v:
        out_shape.append(jax.ShapeDtypeStruct((m, width), F32))
        out_specs.append(pl.BlockSpec((tm, width), lambda i: (i, 0)))
    res = pl.pallas_call(
        functools.partial(_gmlp_kernel, chunk=chunk),
        grid=(m // tm,),
        in_specs=[
            pl.BlockSpec((tm, w2), lambda i: (i, 0)),
            pl.BlockSpec((1, width), lambda i: (0, 0)),
            pl.BlockSpec((1, width), lambda i: (0, 0)),
            pl.BlockSpec(w_mix.shape, lambda i: (0, 0, 0)),
            pl.BlockSpec(b_mix.shape, lambda i: (0, 0, 0)),
        ],
        out_specs=out_specs,
        out_shape=out_shape,
        compiler_params=_cparams(("parallel",), 32),
    )(p_a, ln_g.reshape(1, width), ln_b.reshape(1, width), w_mix, b_mix)
    return res if want_v else (res[0], None)


def _stick_terms(z):
    t = jnp.log(1.0 + jnp.exp(-jnp.abs(z)))
    return jnp.minimum(z, 0.0) - t, jnp.minimum(-z, 0.0) - t


def _suffix_sums(l1m, tri_ref):
    hi, lo = _split_bf16(l1m)
    n = l1m.shape[0]
    both = jnp.dot(jnp.concatenate([hi, lo], axis=0), tri_ref[...], preferred_element_type=F32)
    return both[:n] + both[n:]


def _sb_prompt_kernel(bias_ref, q_ref, k_ref, v_ref, tri_ref, o_ref, *, tq, tk, heads, scale):
    i = pl.program_id(1)
    band = tq // tk
    cols = [slice(h * ATT_HD, (h + 1) * ATT_HD) for h in range(heads)]
    q = [q_ref[0, :, c].astype(BF16) for c in cols]

    def visit(j, h, carry, acc, offset=None):
        start = pl.multiple_of(j * tk, tk)
        k_t = k_ref[0, pl.ds(start, tk), cols[h]].astype(BF16)
        v_t = v_ref[0, pl.ds(start, tk), cols[h]].astype(BF16)
        s = lax.dot_general(q[h], k_t, (((1,), (1,)), ((), ())), preferred_element_type=F32)
        ls, l1m = _stick_terms(s * scale + bias_ref[h])
        if offset is not None:
            row = lax.broadcasted_iota(jnp.int32, (tq, tk), 0)
            col = lax.broadcasted_iota(jnp.int32, (tq, tk), 1)
            vis = col + offset < row
            l1m = jnp.where(vis, l1m, 0.0)
        between = _suffix_sums(l1m, tri_ref)
        if carry is not None:
            between = between + carry
        att = jnp.exp(ls + between)
        if offset is not None:
            att = jnp.where(vis, att, 0.0)
        total = jnp.sum(l1m, axis=1, keepdims=True)
        out = jnp.dot(att.astype(BF16), v_t, preferred_element_type=F32)
        return (total if carry is None else carry + total), (out if acc is None else acc + out)

    state = [None] * (2 * heads)
    for t in reversed(range(band)):
        for h in range(heads):
            state[2 * h], state[2 * h + 1] = visit(i * band + t, h, state[2 * h], state[2 * h + 1], offset=t * tk)

    def body(jj, state):
        new = []
        for h in range(heads):
            new.extend(visit(i * band - 1 - jj, h, state[2 * h], state[2 * h + 1]))
        return tuple(new)

    state = lax.fori_loop(0, i * band, body, tuple(state))
    for h in range(heads):
        o_ref[0, :, cols[h]] = state[2 * h + 1].astype(o_ref.dtype)


def _strict_upper(n):
    r = lax.broadcasted_iota(jnp.int32, (n, n), 0)
    c = lax.broadcasted_iota(jnp.int32, (n, n), 1)
    return (r > c).astype(BF16)


def _sb_prompt(q, k, v, bias, *, tq, tk):
    b, l, w = q.shape
    heads = w // ATT_HD
    assert tq % tk == 0 and l % tq == 0
    return pl.pallas_call(
        functools.partial(_sb_prompt_kernel, tq=tq, tk=tk, heads=heads, scale=ATT_HD ** -0.5),
        grid=(b, l // tq),
        in_specs=[
            pl.BlockSpec(memory_space=pltpu.SMEM),
            pl.BlockSpec((1, tq, w), lambda bi, i: (bi, i, 0)),
            pl.BlockSpec((1, l, w), lambda bi, i: (bi, 0, 0), pipeline_mode=pl.Buffered(1)),
            pl.BlockSpec((1, l, w), lambda bi, i: (bi, 0, 0), pipeline_mode=pl.Buffered(1)),
            pl.BlockSpec((tk, tk), lambda bi, i: (0, 0)),
        ],
        out_specs=pl.BlockSpec((1, tq, w), lambda bi, i: (bi, i, 0)),
        out_shape=jax.ShapeDtypeStruct((b, l, w), BF16),
        compiler_params=_cparams(("parallel", "arbitrary"), 48),
    )(bias, q, k, v, _strict_upper(tk))


def _sb_sample_kernel(pt_ref, bias_ref, qrows_ref, knew_ref, vnew_ref, *rest, pages_per_step, heads, n_q, scale):
    k_refs = rest[:pages_per_step]
    v_refs = rest[pages_per_step:2 * pages_per_step]
    tri_ref, o_ref, carry_scr, acc_scr = rest[2 * pages_per_step:]
    s = pl.program_id(1)
    page = tri_ref.shape[0]
    n_rows = heads * n_q
    q = qrows_ref[0].astype(BF16)

    def block(k_blk, v_blk, masked):
        n_blk = k_blk.shape[0] // page
        sc = lax.dot_general(q, k_blk.astype(BF16), (((1,), (1,)), ((), ())), preferred_element_type=F32)
        z = sc * scale + bias_ref[:, 0:n_blk * page]
        ls, l1m = _stick_terms(z)
        if masked:
            row = lax.broadcasted_iota(jnp.int32, (n_rows, page), 0)
            col = lax.broadcasted_iota(jnp.int32, (n_rows, page), 1)
            vis = col < (row % n_q)
            l1m = jnp.where(vis, l1m, 0.0)
        pages = [slice(p * page, (p + 1) * page) for p in range(n_blk)]
        local = _suffix_sums(jnp.concatenate([l1m[:, sl] for sl in pages], axis=0), tri_ref)
        carry = carry_scr[...]
        between = []
        for p, sl in enumerate(pages):
            between.append(local[p * n_rows:(p + 1) * n_rows] + carry)
            carry = carry + jnp.sum(l1m[:, sl], axis=1, keepdims=True)
        att = jnp.exp(ls + jnp.concatenate(between, axis=1))
        if masked:
            att = jnp.where(vis, att, 0.0)
        acc_scr[...] += jnp.dot(att.astype(BF16), v_blk.astype(BF16), preferred_element_type=F32)
        carry_scr[...] = carry

    @pl.when(s == 0)
    def _():
        carry_scr[...] = jnp.zeros_like(carry_scr)
        acc_scr[...] = jnp.zeros_like(acc_scr)
        block(knew_ref[0], vnew_ref[0], True)

    def heads_to_lanes(ref):
        return jnp.concatenate([ref[pl.ds(h, page, stride=heads), :] for h in range(heads)], axis=1)

    block(jnp.concatenate([heads_to_lanes(r) for r in k_refs], axis=0),
          jnp.concatenate([heads_to_lanes(r) for r in v_refs], axis=0), False)

    @pl.when(s == pl.num_programs(1) - 1)
    def _():
        acc = acc_scr[...]
        for h in range(heads):
            o_ref[0, :, h * ATT_HD:(h + 1) * ATT_HD] = acc[h * n_q:(h + 1) * n_q, h * ATT_HD:(h + 1) * ATT_HD]


def _sb_sample(q, k_new, v_new, bias, cache_k, cache_v, page_table, layer, *, pages_per_step):
    b, n_q, w = q.shape
    heads = w // ATT_HD
    page = cache_k.shape[2] // heads
    n_pages = page_table.shape[1]
    n_rows = heads * n_q
    eye = jnp.eye(heads, dtype=F32)
    qrows = jnp.einsum("bthd,hg->bhtgd", q.reshape(b, n_q, heads, ATT_HD), eye).reshape(b, n_rows, w)
    pad = ((0, 0), (0, page - n_q), (0, 0))
    k_pad = jnp.pad(k_new, pad)
    v_pad = jnp.pad(v_new, pad)
    bias_rows = jnp.broadcast_to(jnp.repeat(bias, n_q)[:, None], (n_rows, page * pages_per_step))

    def page_spec(p):
        def imap(bi, s, pt):
            return (layer, pt[bi, n_pages - 1 - (s * pages_per_step + p)], 0, 0)
        return pl.BlockSpec((None, None, page * heads, ATT_HD), imap)

    grid_spec = pltpu.PrefetchScalarGridSpec(
        num_scalar_prefetch=1,
        grid=(b, n_pages // pages_per_step),
        in_specs=[
            pl.BlockSpec((n_rows, page * pages_per_step), lambda bi, s, pt: (0, 0)),
            pl.BlockSpec((1, n_rows, w), lambda bi, s, pt: (bi, 0, 0)),
            pl.BlockSpec((1, page, w), lambda bi, s, pt: (bi, 0, 0)),
            pl.BlockSpec((1, page, w), lambda bi, s, pt: (bi, 0, 0)),
        ] + [page_spec(p) for p in range(pages_per_step)] * 2 + [
            pl.BlockSpec((page, page), lambda bi, s, pt: (0, 0)),
        ],
        out_specs=pl.BlockSpec((1, n_q, w), lambda bi, s, pt: (bi, 0, 0)),
        scratch_shapes=[pltpu.VMEM((n_rows, 1), F32), pltpu.VMEM((n_rows, w), F32)],
    )
    return pl.pallas_call(
        functools.partial(_sb_sample_kernel, pages_per_step=pages_per_step, heads=heads, n_q=n_q,
                          scale=ATT_HD ** -0.5),
        grid_spec=grid_spec,
        out_shape=jax.ShapeDtypeStruct((b, n_q, w), F32),
        compiler_params=_cparams(("parallel", "arbitrary"), 40),
    )(page_table, bias_rows, qrows, k_pad, v_pad,
      *([cache_k] * pages_per_step), *([cache_v] * pages_per_step), _strict_upper(page))


def _head_sums(x, bd_ref):
    hi, lo = _split_bf16(x)
    return _bd_dot(hi, bd_ref) + _bd_dot(lo, bd_ref)


def _bd_dot(lhs, bd_ref):
    bd = bd_ref[...]
    outs = [jnp.dot(lhs[:, c:c + MXU_DIM], bd, preferred_element_type=F32)
            for c in range(0, lhs.shape[1], MXU_DIM)]
    return jnp.concatenate(outs, axis=1)


def _rwkv_kernel(p_ref, shift0_ref, s0_ref, mu_ref, w0_ref, a0_ref, lora_ref, gup_ref, kk_ref, ka_ref, rk_ref,
                 gng_ref, gnb_ref, bd_ref, bd2_ref, eye_ref, eye2_ref,
                 o_ref, sout_ref,
                 st_scr, carry_scr, xs_scr, kn_scr, r_scr, w_scr, k_scr, v_scr, b_scr, g_scr, bonus_scr, oo_scr,
                 wkn_scr, beta_scr, gamma_scr, *, nb, tc, width):
    ci = pl.program_id(0)
    rows_per_b = RW_HD
    steps = min(tc, SUBLANES)

    @pl.when(ci == 0)
    def _():
        st_scr[...] = s0_ref[...]
        carry_scr[...] = shift0_ref[...]
        xs_scr[...] = jnp.zeros_like(xs_scr)
        kn_scr[...] = jnp.zeros_like(kn_scr)
        if tc % SUBLANES:
            v_scr[...] = jnp.zeros_like(v_scr)

    for b in range(nb):
        pc = p_ref[b]
        xs_scr[SUBLANES:SUBLANES + tc, :] = pc
        xs_scr[SUBLANES - 1:SUBLANES, :] = carry_scr[b:b + 1, :]
        prev = xs_scr[SUBLANES - 1:SUBLANES - 1 + tc, :]
        carry_scr[b:b + 1, :] = pc[tc - 1:tc, :]
        xs = pc + (prev - pc) * mu_ref[...]
        r = xs[:, 0:width]
        k = xs[:, width:2 * width]
        v = xs[:, 2 * width:3 * width]
        xwa = xs[:, 3 * width:3 * width + RW_LORA_W + RW_LORA_A]
        xg = xs[:, 3 * width + RW_LORA_W + RW_LORA_A:]
        lane = lax.broadcasted_iota(jnp.int32, xwa.shape, 1)
        lora_in = jnp.where(lane < RW_LORA_W, jnp.tanh(xwa), xwa).astype(BF16)
        lora = jnp.dot(lora_in, lora_ref[...], preferred_element_type=F32)
        w_log = -_softplus(-(w0_ref[...] + lora[:, :width])) - 0.5
        decay = jnp.exp(-jnp.exp(w_log))
        a = jax.nn.sigmoid(a0_ref[...] + lora[:, width:])
        g = jnp.dot(jax.nn.sigmoid(xg).astype(BF16), gup_ref[...], preferred_element_type=F32)
        kk = k * kk_ref[...]
        kk = kk * lax.rsqrt(jnp.maximum(_head_sums(kk * kk, bd_ref), KK_EPS))
        k2 = k * (1.0 + (a - 1.0) * ka_ref[...])
        kb = kk * a
        r_scr[b, 0:tc, :] = r
        w_scr[b, 0:tc, :] = decay
        k_scr[b, 0:tc, :] = k2
        v_scr[b, 0:tc, :] = v
        kn_scr[b, 0:tc, :] = kk
        b_scr[b, 0:tc, :] = kb
        g_scr[b, 0:tc, :] = g
        bonus_scr[b, 0:tc, :] = _head_sums(r * k2 * rk_ref[...], bd_ref) * v
        kk_next = kn_scr[b, 1:tc + 1, :]
        wkn_scr[b, 0:tc, :] = decay * kk_next
        beta_scr[b, 0:tc, :] = _head_sums(kb * kk_next, bd_ref)
        gamma_scr[b, 0:tc, :] = _head_sums(k2 * kk_next, bd_ref)

    eye = eye_ref[...]
    eye_hi = eye.astype(BF16)
    eye_lo = eye2_ref[...]

    def bcast(row):
        return jnp.broadcast_to(row, (rows_per_b, width))

    def group(gi, _):
        base = pl.multiple_of(gi * SUBLANES, SUBLANES)
        blk = {name: [scr[b, pl.ds(base, steps), :] for b in range(nb)]
               for name, scr in (("r", r_scr), ("w", w_scr), ("k", k_scr), ("kn", kn_scr), ("b", b_scr),
                                 ("wkn", wkn_scr), ("beta", beta_scr), ("gamma", gamma_scr))}
        state = [st_scr[b * rows_per_b:(b + 1) * rows_per_b, :] for b in range(nb)]
        out_rows = [[] for _ in range(nb)]
        nr = rows_per_b
        v_hi, v_lo = [], []
        for b in range(nb):
            v8 = v_scr[b, pl.ds(base, SUBLANES), :]
            hi = v8.astype(BF16).astype(F32)
            lo = v8 - hi
            lane = lax.broadcasted_iota(jnp.int32, lo.shape, 1)
            v_hi.append(hi)
            v_lo.append(jnp.where(lane % RW_HD < RW_HD // 2, pltpu.roll(lo, width - RW_HD // 2, 1),
                                  pltpu.roll(lo, RW_HD // 2, 1)))
        def seg_dot(parts):
            halves = [p[:, c:c + MXU_DIM] for p in parts for c in range(0, width, MXU_DIM)]
            res = jnp.dot(jnp.concatenate(halves, axis=0), bd_ref[...], preferred_element_type=F32)
            nh = width // MXU_DIM
            return [jnp.concatenate([res[(idx * nh + c) * nr:(idx * nh + c + 1) * nr] for c in range(nh)], axis=1)
                    for idx in range(len(parts))]

        def seg_dot_exact(xs):
            blocks = []
            for x in xs:
                hi, lo = _split_bf16(x)
                blocks.extend(jnp.concatenate([hi[:, c:c + MXU_DIM], lo[:, c:c + MXU_DIM]], axis=1)
                              for c in range(0, width, MXU_DIM))
            res = jnp.dot(jnp.concatenate(blocks, axis=0), bd2_ref[...], preferred_element_type=F32)
            nh = width // MXU_DIM
            return [jnp.concatenate([res[(idx * nh + c) * nr:(idx * nh + c + 1) * nr] for c in range(nh)], axis=1)
                    for idx in range(len(xs))]

        def emit(b, o_col):
            out_rows[b].append(jnp.sum(o_col * eye, axis=0, keepdims=True))

        def v_column(b, j):
            return (eye_hi * bcast(v_hi[b][j:j + 1, :].astype(BF16))
                    + eye_lo * bcast(v_lo[b][j:j + 1, :].astype(BF16)))

        pending = [[] for _ in range(nb)]
        for j in range(0, steps, 2):
            row = lambda name, b, jj: bcast(blk[name][b][jj:jj + 1, :])
            for b in range(nb):
                cols = seg_dot([v_column(b, j), v_column(b, j + 1)] + pending[b])
                for o_col in cols[2:]:
                    emit(b, o_col)
                s = state[b]
                s_kk0, s_wkk = seg_dot_exact([s * row("kn", b, j), s * row("wkn", b, j)])
                s1 = s * row("w", b, j) - s_kk0 * row("b", b, j) + cols[0] * row("k", b, j)
                s_kk1 = s_wkk - row("beta", b, j) * s_kk0 + row("gamma", b, j) * cols[0]
                s2 = s1 * row("w", b, j + 1) - s_kk1 * row("b", b, j + 1) + cols[1] * row("k", b, j + 1)
                state[b] = s2
                pending[b] = [(s1 * row("r", b, j)).astype(BF16), (s2 * row("r", b, j + 1)).astype(BF16)]
        for b in range(nb):
            for o_col in seg_dot(pending[b]):
                emit(b, o_col)
        for b in range(nb):
            st_scr[b * rows_per_b:(b + 1) * rows_per_b, :] = state[b]
            oo_scr[b, pl.ds(base, steps), :] = jnp.concatenate(out_rows[b], axis=0)
        return 0

    lax.fori_loop(0, tc // steps, group, 0)

    inv_n = 1.0 / RW_HD
    for b in range(nb):
        o = oo_scr[b, 0:tc, :]
        oc = o - _head_sums(o, bd_ref) * inv_n
        var = _head_sums(oc * oc, bd_ref) * inv_n
        on = oc * lax.rsqrt(var + GN_EPS) * gng_ref[...] + gnb_ref[...] + bonus_scr[b, 0:tc, :]
        o_ref[b] = (on * g_scr[b, 0:tc, :]).astype(o_ref.dtype)

    @pl.when(ci == pl.num_programs(0) - 1)
    def _():
        sout_ref[...] = st_scr[...]


def _rwkv(p_c, shift0, wkv0, mu, w0, w_up, a0, a_up, g_up, k_k, k_a, r_k, gn_g, gn_b, *, tc):
    nb, l, cs = p_c.shape
    width = w0.shape[0]
    heads = width // RW_HD
    s0 = jnp.transpose(wkv0.astype(F32), (0, 2, 1, 3)).reshape(nb * RW_HD, width)
    lora = jnp.zeros((RW_LORA_W + RW_LORA_A, 2 * width), F32)
    lora = lora.at[:RW_LORA_W, :width].set(w_up).at[RW_LORA_W:, width:].set(a_up).astype(BF16)
    lane = jnp.arange(MXU_DIM)
    bd = (lane[:, None] // RW_HD == lane[None, :] // RW_HD).astype(BF16)
    lane_k = jnp.arange(width)[None, :] % RW_HD
    eye = (jnp.arange(RW_HD)[:, None] == lane_k).astype(F32)
    eye2 = ((jnp.arange(RW_HD)[:, None] + RW_HD // 2) % RW_HD == lane_k).astype(BF16)
    row = lambda t: t.reshape(1, -1).astype(F32)
    full = lambda arr: pl.BlockSpec(arr.shape, lambda c: (0,) * arr.ndim)
    small = [shift0.astype(F32), s0, row(mu), row(w0), row(a0), lora, g_up.astype(BF16), row(k_k), row(k_a),
             row(r_k), row(gn_g), row(gn_b), bd, jnp.concatenate([bd, bd], axis=0), eye, eye2]
    tc_pad = -(-tc // SUBLANES) * SUBLANES
    chunk_scr = pltpu.VMEM((nb, tc_pad, width), F32)
    assert min(tc, SUBLANES) % 2 == 0
    out, s_out = pl.pallas_call(
        functools.partial(_rwkv_kernel, nb=nb, tc=tc, width=width),
        grid=(l // tc,),
        in_specs=[pl.BlockSpec((nb, tc, cs), lambda c: (0, c, 0))] + [full(t) for t in small],
        out_specs=[pl.BlockSpec((nb, tc, width), lambda c: (0, c, 0)),
                   pl.BlockSpec((nb * RW_HD, width), lambda c: (0, 0))],
        out_shape=[jax.ShapeDtypeStruct((nb, l, width), BF16),
                   jax.ShapeDtypeStruct((nb * RW_HD, width), F32)],
        scratch_shapes=[pltpu.VMEM((nb * RW_HD, width), F32), pltpu.VMEM((nb, cs), F32),
                        pltpu.VMEM((SUBLANES + tc, cs), F32),
                        pltpu.VMEM((nb, tc_pad + SUBLANES, width), F32)] + [chunk_scr] * 11,
        compiler_params=_cparams(("arbitrary",), 40),
    )(p_c, *small)
    s_new = jnp.transpose(s_out.reshape(nb, RW_HD, heads, RW_HD), (0, 2, 1, 3))
    return out, s_new


def _conv_kernel(p_ref, c0_ref, w_ref, cb_ref, lng_ref, lnb_ref, o_ref, c1_ref, zc_scr, *, tm, taps):
    i = pl.program_id(1)
    width = o_ref.shape[2]
    first = CONV_HALO - (taps - 1)

    @pl.when(i == 0)
    def _():
        zc_scr[0:first, :] = jnp.zeros((first, width), F32)
        zc_scr[first:CONV_HALO, :] = c0_ref[0]

    @pl.when(i > 0)
    def _():
        zc_scr[0:CONV_HALO, :] = zc_scr[tm:tm + CONV_HALO, :]

    p = p_ref[0]
    zc_scr[CONV_HALO:CONV_HALO + tm, :] = p[:, :width] * jax.nn.sigmoid(p[:, width:])
    y = jnp.zeros((tm, width), F32)
    for j in range(taps):
        y = y + zc_scr[first + j:first + j + tm, :] * w_ref[j:j + 1, :]
    y = y + cb_ref[...]
    yc = y - jnp.mean(y, axis=-1, keepdims=True)
    var = jnp.mean(yc * yc, axis=-1, keepdims=True)
    yn = yc * lax.rsqrt(var + LN_EPS) * lng_ref[...] + lnb_ref[...]
    o_ref[0] = (yn * jax.nn.sigmoid(yn)).astype(o_ref.dtype)

    @pl.when(i == pl.num_programs(1) - 1)
    def _():
        c1_ref[0] = zc_scr[tm + first:tm + CONV_HALO, :]


def _conv(p_d, conv0, conv_w, conv_b, ln_g, ln_b, *, tm):
    nb, l, w2 = p_d.shape
    width = w2 // 2
    taps = conv_w.shape[0]
    row = lambda t: t.reshape(1, width).astype(F32)
    return pl.pallas_call(
        functools.partial(_conv_kernel, tm=tm, taps=taps),
        grid=(nb, l // tm),
        in_specs=[
            pl.BlockSpec((1, tm, w2), lambda b, i: (b, i, 0)),
            pl.BlockSpec((1, taps - 1, width), lambda b, i: (b, 0, 0)),
            pl.BlockSpec((taps, width), lambda b, i: (0, 0)),
            pl.BlockSpec((1, width), lambda b, i: (0, 0)),
            pl.BlockSpec((1, width), lambda b, i: (0, 0)),
            pl.BlockSpec((1, width), lambda b, i: (0, 0)),
        ],
        out_specs=[pl.BlockSpec((1, tm, width), lambda b, i: (b, i, 0)),
                   pl.BlockSpec((1, taps - 1, width), lambda b, i: (b, 0, 0))],
        out_shape=[jax.ShapeDtypeStruct((nb, l, width), BF16),
                   jax.ShapeDtypeStruct((nb, taps - 1, width), F32)],
        scratch_shapes=[pltpu.VMEM((CONV_HALO + tm, width), F32)],
        compiler_params=_cparams(("parallel", "arbitrary"), 32),
    )(p_d, conv0.astype(F32), conv_w.astype(F32), row(conv_b), row(ln_g), row(ln_b))


def _merge_kernel(pa_ref, pb_ref, pc_ref, pd_ref, wa_ref, wb_ref, wc_ref, wd_ref,
                  ga_ref, gb_ref, gc_ref, gd_ref, o_ref):
    acc = None
    for p_ref, w_ref, g_ref in ((pa_ref, wa_ref, ga_ref), (pb_ref, wb_ref, gb_ref),
                                (pc_ref, wc_ref, gc_ref), (pd_ref, wd_ref, gd_ref)):
        y = g_ref[...].astype(F32) * jnp.dot(p_ref[...], w_ref[...], preferred_element_type=F32)
        acc = y if acc is None else acc + y
    o_ref[...] = acc.astype(o_ref.dtype)


def _merge(pres, outs, gates, *, tm, tn):
    m, width = pres[0].shape
    d = outs[0].shape[1]
    nj = d // tn
    gate_spec = lambda bidx: pl.BlockSpec((tm, tn), lambda i, j: (i, bidx * nj + j))
    return pl.pallas_call(
        _merge_kernel,
        grid=(m // tm, nj),
        in_specs=[pl.BlockSpec((tm, width), lambda i, j: (i, 0))] * 4
                 + [pl.BlockSpec((width, tn), lambda i, j: (0, j))] * 4
                 + [gate_spec(bidx) for bidx in range(4)],
        out_specs=pl.BlockSpec((tm, tn), lambda i, j: (i, j)),
        out_shape=jax.ShapeDtypeStruct((m, d), BF16),
        compiler_params=_cparams(("parallel", "arbitrary"), 32),
    )(*pres, *outs, gates, gates, gates, gates)


def _pick_tile(m, pref):
    return pref if m % pref == 0 else m


def _trunk_layer(x, lw, attend, shift0, wkv0, conv0, *, gmlp_chunk, want_v):
    nb, l, d = x.shape
    m = nb * l
    x2 = x.reshape(m, d)
    tm = _pick_tile(m, 1024)
    tm_s = _pick_tile(m, 512)
    p_a, q, k, v, p_c, p_d = _in_proj(
        x2, lw["norm_mix"], lw["w_branch"], lw["b_qn"], lw["b_kn"],
        ("plain", "qnorm", "knorm", "plain", "plain", "plain"), lw["in_widths"][:-1], tm=tm, tn=MXU_DIM)
    (gates,) = _in_proj(x2, lw["norm_mix"], lw["w_gate"], lw["b_qn"], lw["b_kn"], ("sigmoid",),
                        lw["in_widths"][-1:], tm=tm, tn=2 * MXU_DIM)

    width = q.shape[1]
    pre_a, v_a = _gmlp(p_a, lw["a_ln_g"], lw["a_ln_b"], lw["a_mix"], lw["a_bias"],
                       tm=_pick_tile(m, 256), chunk=gmlp_chunk, want_v=want_v)
    pre_b = attend(q.reshape(nb, l, width), k.reshape(nb, l, width), v.reshape(nb, l, width))
    pre_c, wkv1 = _rwkv(p_c.reshape(nb, l, -1), shift0, wkv0, lw["c_mu"], lw["c_w0"], lw["c_w_up"], lw["c_a0"],
                        lw["c_a_up"], lw["c_g_up"], lw["c_k_k"], lw["c_k_a"], lw["c_r_k"], lw["c_gn_g"],
                        lw["c_gn_b"], tc=min(l, 128))
    shift1 = p_c.reshape(nb, l, -1)[:, -1]
    pre_d, conv1 = _conv(p_d.reshape(nb, l, -1), conv0, lw["d_conv_w"], lw["d_conv_b"], lw["d_ln_g"], lw["d_ln_b"],
                         tm=min(l, 256))
    merged = _merge([pre_a, pre_b.reshape(m, width).astype(BF16), pre_c.reshape(m, width), pre_d.reshape(m, width)],
                    [lw["a_out"], lw["b_out"], lw["c_out"], lw["d_out"]], gates, tm=tm_s, tn=512)
    x2 = _matmul_residual(merged, lw["w_mix_out"], x2, tm=tm_s, tn=512)
    x2 = _ffn(x2, lw["norm_ffn"], lw["f_gate"], lw["f_up"], lw["f_down"], tm=tm_s, th=512)
    return x2.reshape(nb, l, d), k, v, wkv1, shift1, conv1, v_a


def kernel(x_prompt, x_sample, cache_k, cache_v, page_table, state_wkv, state_shift, state_conv, norm_mix, w_in, a_ln_g, a_ln_b, a_ws, a_bs, a_out, b_qn, b_kn, b_bias, b_out, c_mu, c_w0, c_w_up, c_a0, c_a_up, c_g_up, c_k_k, c_k_a, c_r_k, c_gn_g, c_gn_b, c_out, d_conv_w, d_conv_b, d_ln_g, d_ln_b, d_out, w_mix_out, norm_ffn, f_gate, f_up, f_down):
    depth = w_in.shape[0]
    bp, seq, d_model = x_prompt.shape
    bs, dec = x_sample.shape[:2]
    a_w = a_ln_g.shape[1]
    bw = b_out.shape[1]
    heads = b_bias.shape[1]
    c_shift = c_mu.shape[1]
    d_w = d_ln_g.shape[1]
    chunk = a_ws.shape[2]
    taps = d_conv_w.shape[1]
    gw = a_w // A_GROUPS
    n_phys, page = cache_k.shape[1:3]
    cache_k = cache_k.reshape(depth, n_phys, page * heads, ATT_HD)
    cache_v = cache_v.reshape(depth, n_phys, page * heads, ATT_HD)
    in_widths = (2 * a_w, bw, bw, bw, c_shift, 2 * d_w, 4 * d_model)
    causal = jnp.tril(jnp.ones((chunk, chunk), dtype=bool))
    causal_s = jnp.tril(jnp.ones((dec, dec), dtype=bool))

    yp, ys = x_prompt, x_sample
    outs = [[] for _ in range(11)]
    for l in range(depth):
        lw = {
            "norm_mix": norm_mix[l], "w_branch": w_in[l][:, :-in_widths[-1]].astype(BF16),
            "w_gate": w_in[l][:, -in_widths[-1]:].astype(BF16), "in_widths": in_widths,
            "a_ln_g": a_ln_g[l], "a_ln_b": a_ln_b[l], "a_out": a_out[l].astype(BF16),
            "b_qn": b_qn[l], "b_kn": b_kn[l], "b_out": b_out[l].astype(BF16),
            "c_mu": c_mu[l], "c_w0": c_w0[l], "c_w_up": c_w_up[l], "c_a0": c_a0[l], "c_a_up": c_a_up[l],
            "c_g_up": c_g_up[l], "c_k_k": c_k_k[l], "c_k_a": c_k_a[l], "c_r_k": c_r_k[l],
            "c_gn_g": c_gn_g[l], "c_gn_b": c_gn_b[l], "c_out": c_out[l].astype(BF16),
            "d_conv_w": d_conv_w[l], "d_conv_b": d_conv_b[l], "d_ln_g": d_ln_g[l], "d_ln_b": d_ln_b[l],
            "d_out": d_out[l].astype(BF16),
            "w_mix_out": w_mix_out[l].astype(BF16), "norm_ffn": norm_ffn[l],
            "f_gate": f_gate[l].astype(BF16), "f_up": f_up[l].astype(BF16), "f_down": f_down[l].astype(BF16),
        }
        lw_p = dict(lw)
        lw_p["a_mix"] = jnp.where(causal, a_ws[l], 0.0).astype(BF16)
        lw_p["a_bias"] = jnp.broadcast_to(a_bs[l][:, :, None], (A_GROUPS, chunk, gw)).astype(F32)
        attend_p = functools.partial(_sb_prompt, bias=b_bias[l], tq=1024, tk=256)
        yp, kp, vp, wp, sp, cp, _ = _trunk_layer(
            yp, lw_p, attend_p, jnp.zeros((bp, c_shift), F32), jnp.zeros((bp, c_w0.shape[1] // RW_HD, RW_HD, RW_HD), F32),
            jnp.zeros((bp, taps - 1, d_w), F32), gmlp_chunk=chunk, want_v=False)

        lw_s = dict(lw)
        w_small = jnp.where(causal_s, a_ws[l][:, :dec, :dec], 0.0)
        lw_s["a_mix"] = jnp.einsum("ab,gts->gatbs", jnp.eye(bs, dtype=F32), w_small).reshape(
            A_GROUPS, bs * dec, bs * dec).astype(BF16)
        lw_s["a_bias"] = jnp.broadcast_to(jnp.tile(a_bs[l][:, :dec], (1, bs))[:, :, None],
                                          (A_GROUPS, bs * dec, gw)).astype(F32)
        attend_s = lambda q, k, v: _sb_sample(q, k, v, b_bias[l], cache_k, cache_v, page_table, l, pages_per_step=8)
        ys, ksn, vsn, wsn, ssn, csn, gvs = _trunk_layer(
            ys, lw_s, attend_s, state_shift[l], state_wkv[l], state_conv[l], gmlp_chunk=bs * dec, want_v=True)

        layer_out = (kp.reshape(bp, seq, heads, ATT_HD), vp.reshape(bp, seq, heads, ATT_HD),
                     ksn.reshape(bs, dec, heads, ATT_HD), vsn.reshape(bs, dec, heads, ATT_HD),
                     wp, wsn, sp, ssn, cp, csn, gvs.reshape(bs, dec, a_w))
        for acc, val in zip(outs, layer_out):
            acc.append(val)
    return (yp, ys) + tuple(jnp.stack(o) for o in outs)
```

```python
import functools

import jax
import jax.numpy as jnp
from jax import lax
from jax.experimental import pallas as pl
from jax.experimental.pallas import tpu as pltpu

F32 = jnp.float32
BF16 = jnp.bfloat16

RMS_EPS = 1e-6
LN_EPS = 1e-5
GN_EPS = 64e-5
KK_EPS = 1e-24

LANES = 128
SUBLANES = 8
MXU_DIM = 256
MIB = 1024 * 1024

A_GROUPS = 4
ATT_HD = 128
RW_HD = 64
RW_LORA_W = 64
RW_LORA_A = 64
RW_LORA_G = 128
CONV_HALO = 32


def _cparams(semantics, vmem_mib):
    return pltpu.CompilerParams(dimension_semantics=semantics, vmem_limit_bytes=vmem_mib * MIB)


def _split_bf16(x):
    hi = x.astype(BF16)
    lo = (x - hi.astype(F32)).astype(BF16)
    return hi, lo


def _softplus(y):
    return jnp.maximum(y, 0.0) + jnp.log1p(jnp.exp(-jnp.abs(y)))


def _in_proj_kernel(x_ref, g_ref, w_ref, qg_ref, kg_ref, *rest, segments):
    out_refs = rest[:len(segments)]
    h_scr = rest[len(segments)]
    j = pl.program_id(1)

    @pl.when(j == 0)
    def _():
        x = x_ref[...]
        ms = jnp.mean(x * x, axis=-1, keepdims=True)
        h_scr[...] = (x * lax.rsqrt(ms + RMS_EPS) * g_ref[...]).astype(BF16)

    acc = jnp.dot(h_scr[...], w_ref[...].astype(BF16), preferred_element_type=F32)
    for (kind, start, stop), o_ref in zip(segments, out_refs):
        in_segment = pl.when((j >= start) & (j < stop)) if len(segments) > 1 else (lambda f: f())

        @in_segment
        def _(kind=kind, o_ref=o_ref):
            if kind == "sigmoid":
                o_ref[...] = jax.nn.sigmoid(acc).astype(o_ref.dtype)
            elif kind in ("qnorm", "knorm"):
                hg_ref = qg_ref if kind == "qnorm" else kg_ref
                for c in range(acc.shape[1] // ATT_HD):
                    sl = slice(c * ATT_HD, (c + 1) * ATT_HD)
                    y = acc[:, sl]
                    ms = jnp.mean(y * y, axis=-1, keepdims=True)
                    o_ref[:, sl] = y * lax.rsqrt(ms + RMS_EPS) * hg_ref[...]
            else:
                o_ref[...] = acc


def _in_proj(x, gain, w_all, layer, col0, q_gain, k_gain, seg_kinds, seg_widths, *, tm, tn):
    m, k = x.shape
    n = sum(seg_widths)
    assert col0 % tn == 0
    tile0 = col0 // tn
    segments, out_specs, out_shape = [], [], []
    start = 0
    for kind, width in zip(seg_kinds, seg_widths):
        tiles = width // tn
        assert tiles * tn == width
        segments.append((kind, start, start + tiles))
        out_specs.append(pl.BlockSpec((tm, tn), lambda i, j, s=start, t=tiles: (i, jnp.clip(j - s, 0, t - 1))))
        out_shape.append(jax.ShapeDtypeStruct((m, width), BF16 if kind == "sigmoid" else F32))
        start += tiles
    assert start * tn == n
    return pl.pallas_call(
        functools.partial(_in_proj_kernel, segments=tuple(segments)),
        grid=(m // tm, n // tn),
        in_specs=[
            pl.BlockSpec((tm, k), lambda i, j: (i, 0)),
            pl.BlockSpec((1, k), lambda i, j: (0, 0)),
            pl.BlockSpec((None, k, tn), lambda i, j: (layer, 0, tile0 + j)),
            pl.BlockSpec((1, ATT_HD), lambda i, j: (0, 0)),
            pl.BlockSpec((1, ATT_HD), lambda i, j: (0, 0)),
        ],
        out_specs=out_specs,
        out_shape=out_shape,
        scratch_shapes=[pltpu.VMEM((tm, k), BF16)],
        compiler_params=_cparams(("parallel", "arbitrary"), 48),
    )(x, gain.reshape(1, k), w_all, q_gain.reshape(1, ATT_HD), k_gain.reshape(1, ATT_HD))


def _matmul_res_kernel(a_ref, w_ref, r_ref, o_ref):
    o_ref[...] = r_ref[...] + jnp.dot(a_ref[...], w_ref[...], preferred_element_type=F32)


def _matmul_residual(a, w, res, *, tm):
    m, k = a.shape
    n = w.shape[1]
    return pl.pallas_call(
        _matmul_res_kernel,
        grid=(m // tm,),
        in_specs=[
            pl.BlockSpec((tm, k), lambda i: (i, 0)),
            pl.BlockSpec((k, n), lambda i: (0, 0), pipeline_mode=pl.Buffered(1)),
            pl.BlockSpec((tm, n), lambda i: (i, 0)),
        ],
        out_specs=pl.BlockSpec((tm, n), lambda i: (i, 0)),
        out_shape=jax.ShapeDtypeStruct((m, n), F32),
        compiler_params=_cparams(("parallel",), 40),
    )(a, w, res)


def _ffn_kernel(x_ref, g_ref, wg_ref, wu_ref, wd_ref, o_ref, h_scr, acc_scr):
    j = pl.program_id(1)

    @pl.when(j == 0)
    def _():
        x = x_ref[...]
        ms = jnp.mean(x * x, axis=-1, keepdims=True)
        h_scr[...] = (x * lax.rsqrt(ms + RMS_EPS) * g_ref[...]).astype(BF16)
        acc_scr[...] = jnp.zeros_like(acc_scr)

    h = h_scr[...]
    gate = jnp.dot(h, wg_ref[...], preferred_element_type=F32)
    up = jnp.dot(h, wu_ref[...], preferred_element_type=F32)
    act = (gate * jax.nn.sigmoid(gate) * up).astype(BF16)
    acc_scr[...] += jnp.dot(act, wd_ref[...], preferred_element_type=F32)

    @pl.when(j == pl.num_programs(1) - 1)
    def _():
        o_ref[...] = x_ref[...] + acc_scr[...]


def _ffn(x, gain, wg, wu, wd, *, tm, th):
    m, d = x.shape
    hidden = wg.shape[1]
    return pl.pallas_call(
        _ffn_kernel,
        grid=(m // tm, hidden // th),
        in_specs=[
            pl.BlockSpec((tm, d), lambda i, j: (i, 0)),
            pl.BlockSpec((1, d), lambda i, j: (0, 0)),
            pl.BlockSpec((d, th), lambda i, j: (0, j)),
            pl.BlockSpec((d, th), lambda i, j: (0, j)),
            pl.BlockSpec((th, d), lambda i, j: (j, 0)),
        ],
        out_specs=pl.BlockSpec((tm, d), lambda i, j: (i, 0)),
        out_shape=jax.ShapeDtypeStruct((m, d), F32),
        scratch_shapes=[pltpu.VMEM((tm, d), BF16), pltpu.VMEM((tm, d), F32)],
        compiler_params=_cparams(("parallel", "arbitrary"), 52),
    )(x, gain.reshape(1, d), wg, wu, wd)


def _gmlp_kernel(p_ref, lng_ref, lnb_ref, wm_ref, bs_ref, o_ref, *v_out, chunk):
    z = jax.nn.gelu(p_ref[...])
    width = z.shape[1] // 2
    u = z[:, :width]
    v = z[:, width:]
    vc = v - jnp.mean(v, axis=-1, keepdims=True)
    var = jnp.mean(vc * vc, axis=-1, keepdims=True)
    v = vc * lax.rsqrt(var + LN_EPS) * lng_ref[...] + lnb_ref[...]
    if v_out:
        v_out[0][...] = v
    gw = width // A_GROUPS
    for ci in range(z.shape[0] // chunk):
        rows = slice(ci * chunk, (ci + 1) * chunk)
        for g in range(A_GROUPS):
            cols = slice(g * gw, (g + 1) * gw)
            s = jnp.dot(wm_ref[g], v[rows, cols].astype(BF16), preferred_element_type=F32) + bs_ref[g]
            o_ref[rows, cols] = (u[rows, cols] * s).astype(o_ref.dtype)


def _gmlp(p_a, ln_g, ln_b, w_mix, b_mix, *, tm, chunk, want_v):
    m, w2 = p_a.shape
    width = w2 // 2
    out_shape = [jax.ShapeDtypeStruct((m, width), BF16)]
    out_specs = [pl.BlockSpec((tm, width), lambda i: (i, 0))]
    if want_v:
        out_shape.append(jax.ShapeDtypeStruct((m, width), F32))
        out_specs.append(pl.BlockSpec((tm, width), lambda i: (i, 0)))
    res = pl.pallas_call(
        functools.partial(_gmlp_kernel, chunk=chunk),
        grid=(m // tm,),
        in_specs=[
            pl.BlockSpec((tm, w2), lambda i: (i, 0)),
            pl.BlockSpec((1, width), lambda i: (0, 0)),
            pl.BlockSpec((1, width), lambda i: (0, 0)),
            pl.BlockSpec(w_mix.shape, lambda i: (0, 0, 0)),
            pl.BlockSpec(b_mix.shape, lambda i: (0, 0, 0)),
        ],
        out_specs=out_specs,
        out_shape=out_shape,
        compiler_params=_cparams(("parallel",), 32),
    )(p_a, ln_g.reshape(1, width), ln_b.reshape(1, width), w_mix, b_mix)
    return res if want_v else (res[0], None)


def _stick_terms(z):
    t = jnp.log(1.0 + jnp.exp(-jnp.abs(z)))
    return jnp.minimum(z, 0.0) - t, jnp.minimum(-z, 0.0) - t


def _suffix_sums(l1m, tri_ref):
    hi, lo = _split_bf16(l1m)
    n = l1m.shape[0]
    both = jnp.dot(jnp.concatenate([hi, lo], axis=0), tri_ref[...], preferred_element_type=F32)
    return both[:n] + both[n:]


def _sb_prompt_kernel(bias_ref, q_ref, k_ref, v_ref, tri_ref, o_ref, *, tq, tk, heads, scale):
    i = pl.program_id(1)
    band = tq // tk
    cols = [slice(h * ATT_HD, (h + 1) * ATT_HD) for h in range(heads)]
    q = [q_ref[0, :, c].astype(BF16) for c in cols]

    def visit(j, h, carry, acc, offset=None):
        start = pl.multiple_of(j * tk, tk)
        k_t = k_ref[0, pl.ds(start, tk), cols[h]].astype(BF16)
        v_t = v_ref[0, pl.ds(start, tk), cols[h]].astype(BF16)
        s = lax.dot_general(q[h], k_t, (((1,), (1,)), ((), ())), preferred_element_type=F32)
        ls, l1m = _stick_terms(s * scale + bias_ref[h])
        if offset is not None:
            row = lax.broadcasted_iota(jnp.int32, (tq, tk), 0)
            col = lax.broadcasted_iota(jnp.int32, (tq, tk), 1)
            vis = col + offset < row
            l1m = jnp.where(vis, l1m, 0.0)
        between = _suffix_sums(l1m, tri_ref)
        if carry is not None:
            between = between + carry
        att = jnp.exp(ls + between)
        if offset is not None:
            att = jnp.where(vis, att, 0.0)
        total = jnp.sum(l1m, axis=1, keepdims=True)
        out = jnp.dot(att.astype(BF16), v_t, preferred_element_type=F32)
        return (total if carry is None else carry + total), (out if acc is None else acc + out)

    state = [None] * (2 * heads)
    for t in reversed(range(band)):
        for h in range(heads):
            state[2 * h], state[2 * h + 1] = visit(i * band + t, h, state[2 * h], state[2 * h + 1], offset=t * tk)

    def body(jj, state):
        new = []
        for h in range(heads):
            new.extend(visit(i * band - 1 - jj, h, state[2 * h], state[2 * h + 1]))
        return tuple(new)

    state = lax.fori_loop(0, i * band, body, tuple(state))
    for h in range(heads):
        o_ref[0, :, cols[h]] = state[2 * h + 1].astype(o_ref.dtype)


def _strict_upper(n):
    r = lax.broadcasted_iota(jnp.int32, (n, n), 0)
    c = lax.broadcasted_iota(jnp.int32, (n, n), 1)
    return (r > c).astype(BF16)


def _sb_prompt(q, k, v, bias, *, tq, tk):
    b, l, w = q.shape
    heads = w // ATT_HD
    assert tq % tk == 0 and l % tq == 0
    return pl.pallas_call(
        functools.partial(_sb_prompt_kernel, tq=tq, tk=tk, heads=heads, scale=ATT_HD ** -0.5),
        grid=(b, l // tq),
        in_specs=[
            pl.BlockSpec(memory_space=pltpu.SMEM),
            pl.BlockSpec((1, tq, w), lambda bi, i: (bi, i, 0)),
            pl.BlockSpec((1, l, w), lambda bi, i: (bi, 0, 0), pipeline_mode=pl.Buffered(1)),
            pl.BlockSpec((1, l, w), lambda bi, i: (bi, 0, 0), pipeline_mode=pl.Buffered(1)),
            pl.BlockSpec((tk, tk), lambda bi, i: (0, 0)),
        ],
        out_specs=pl.BlockSpec((1, tq, w), lambda bi, i: (bi, i, 0)),
        out_shape=jax.ShapeDtypeStruct((b, l, w), BF16),
        compiler_params=_cparams(("parallel", "arbitrary"), 48),
    )(bias, q, k, v, _strict_upper(tk))


def _sb_sample_kernel(pt_ref, bias_ref, qrows_ref, knew_ref, vnew_ref, *rest, pages_per_step, heads, n_q, scale):
    k_refs = rest[:pages_per_step]
    v_refs = rest[pages_per_step:2 * pages_per_step]
    tri_ref, o_ref, carry_scr, acc_scr = rest[2 * pages_per_step:]
    s = pl.program_id(1)
    page = tri_ref.shape[0]
    n_rows = heads * n_q
    q = qrows_ref[0].astype(BF16)

    def block(k_blk, v_blk, masked):
        n_blk = k_blk.shape[0] // page
        sc = lax.dot_general(q, k_blk.astype(BF16), (((1,), (1,)), ((), ())), preferred_element_type=F32)
        z = sc * scale + bias_ref[:, 0:n_blk * page]
        ls, l1m = _stick_terms(z)
        if masked:
            row = lax.broadcasted_iota(jnp.int32, (n_rows, page), 0)
            col = lax.broadcasted_iota(jnp.int32, (n_rows, page), 1)
            vis = col < (row % n_q)
            l1m = jnp.where(vis, l1m, 0.0)
        pages = [slice(p * page, (p + 1) * page) for p in range(n_blk)]
        local = _suffix_sums(jnp.concatenate([l1m[:, sl] for sl in pages], axis=0), tri_ref)
        carry = carry_scr[...]
        between = []
        for p, sl in enumerate(pages):
            between.append(local[p * n_rows:(p + 1) * n_rows] + carry)
            carry = carry + jnp.sum(l1m[:, sl], axis=1, keepdims=True)
        att = jnp.exp(ls + jnp.concatenate(between, axis=1))
        if masked:
            att = jnp.where(vis, att, 0.0)
        acc_scr[...] += jnp.dot(att.astype(BF16), v_blk.astype(BF16), preferred_element_type=F32)
        carry_scr[...] = carry

    @pl.when(s == 0)
    def _():
        carry_scr[...] = jnp.zeros_like(carry_scr)
        acc_scr[...] = jnp.zeros_like(acc_scr)
        block(knew_ref[0], vnew_ref[0], True)

    def heads_to_lanes(ref):
        return jnp.concatenate([ref[pl.ds(h, page, stride=heads), :] for h in range(heads)], axis=1)

    block(jnp.concatenate([heads_to_lanes(r) for r in k_refs], axis=0),
          jnp.concatenate([heads_to_lanes(r) for r in v_refs], axis=0), False)

    @pl.when(s == pl.num_programs(1) - 1)
    def _():
        acc = acc_scr[...]
        for h in range(heads):
            o_ref[0, :, h * ATT_HD:(h + 1) * ATT_HD] = acc[h * n_q:(h + 1) * n_q, h * ATT_HD:(h + 1) * ATT_HD]


def _sb_sample(q, k_new, v_new, bias, cache_k, cache_v, page_table, layer, *, pages_per_step):
    b, n_q, w = q.shape
    heads = w // ATT_HD
    page = cache_k.shape[2] // heads
    n_pages = page_table.shape[1]
    n_rows = heads * n_q
    eye = jnp.eye(heads, dtype=F32)
    qrows = jnp.einsum("bthd,hg->bhtgd", q.reshape(b, n_q, heads, ATT_HD), eye).reshape(b, n_rows, w)
    pad = ((0, 0), (0, page - n_q), (0, 0))
    k_pad = jnp.pad(k_new, pad)
    v_pad = jnp.pad(v_new, pad)
    bias_rows = jnp.broadcast_to(jnp.repeat(bias, n_q)[:, None], (n_rows, page * pages_per_step))

    def page_spec(p):
        def imap(bi, s, pt):
            return (layer, pt[bi, n_pages - 1 - (s * pages_per_step + p)], 0, 0)
        return pl.BlockSpec((None, None, page * heads, ATT_HD), imap)

    grid_spec = pltpu.PrefetchScalarGridSpec(
        num_scalar_prefetch=1,
        grid=(b, n_pages // pages_per_step),
        in_specs=[
            pl.BlockSpec((n_rows, page * pages_per_step), lambda bi, s, pt: (0, 0)),
            pl.BlockSpec((1, n_rows, w), lambda bi, s, pt: (bi, 0, 0)),
            pl.BlockSpec((1, page, w), lambda bi, s, pt: (bi, 0, 0)),
            pl.BlockSpec((1, page, w), lambda bi, s, pt: (bi, 0, 0)),
        ] + [page_spec(p) for p in range(pages_per_step)] * 2 + [
            pl.BlockSpec((page, page), lambda bi, s, pt: (0, 0)),
        ],
        out_specs=pl.BlockSpec((1, n_q, w), lambda bi, s, pt: (bi, 0, 0)),
        scratch_shapes=[pltpu.VMEM((n_rows, 1), F32), pltpu.VMEM((n_rows, w), F32)],
    )
    return pl.pallas_call(
        functools.partial(_sb_sample_kernel, pages_per_step=pages_per_step, heads=heads, n_q=n_q,
                          scale=ATT_HD ** -0.5),
        grid_spec=grid_spec,
        out_shape=jax.ShapeDtypeStruct((b, n_q, w), F32),
        compiler_params=_cparams(("parallel", "arbitrary"), 40),
    )(page_table, bias_rows, qrows, k_pad, v_pad,
      *([cache_k] * pages_per_step), *([cache_v] * pages_per_step), _strict_upper(page))


def _head_sums(x, bd_ref):
    hi, lo = _split_bf16(x)
    return _bd_dot(hi, bd_ref) + _bd_dot(lo, bd_ref)


def _bd_dot(lhs, bd_ref):
    bd = bd_ref[...]
    outs = [jnp.dot(lhs[:, c:c + MXU_DIM], bd, preferred_element_type=F32)
            for c in range(0, lhs.shape[1], MXU_DIM)]
    return jnp.concatenate(outs, axis=1)


def _rwkv_kernel(p_ref, shift0_ref, s0_ref, mu_ref, w0_ref, a0_ref, lora_ref, gup_ref, kk_ref, ka_ref, rk_ref,
                 gng_ref, gnb_ref, bd_ref, eye_ref, eye2_ref,
                 o_ref, sout_ref,
                 st_scr, carry_scr, xs_scr, kn_scr, r_scr, w_scr, k_scr, v_scr, b_scr, g_scr, bonus_scr, oo_scr,
                 *, nb, tc, width):
    ci = pl.program_id(0)
    rows_per_b = RW_HD
    steps = min(tc, SUBLANES)

    @pl.when(ci == 0)
    def _():
        st_scr[...] = s0_ref[...]
        carry_scr[...] = shift0_ref[...]
        xs_scr[...] = jnp.zeros_like(xs_scr)
        if tc % SUBLANES:
            v_scr[...] = jnp.zeros_like(v_scr)

    for b in range(nb):
        pc = p_ref[b]
        xs_scr[SUBLANES:SUBLANES + tc, :] = pc
        xs_scr[SUBLANES - 1:SUBLANES, :] = carry_scr[b:b + 1, :]
        prev = xs_scr[SUBLANES - 1:SUBLANES - 1 + tc, :]
        carry_scr[b:b + 1, :] = pc[tc - 1:tc, :]
        xs = pc + (prev - pc) * mu_ref[...]
        r = xs[:, 0:width]
        k = xs[:, width:2 * width]
        v = xs[:, 2 * width:3 * width]
        xwa = xs[:, 3 * width:3 * width + RW_LORA_W + RW_LORA_A]
        xg = xs[:, 3 * width + RW_LORA_W + RW_LORA_A:]
        lane = lax.broadcasted_iota(jnp.int32, xwa.shape, 1)
        lora_in = jnp.where(lane < RW_LORA_W, jnp.tanh(xwa), xwa).astype(BF16)
        lora = jnp.dot(lora_in, lora_ref[...], preferred_element_type=F32)
        w_log = -_softplus(-(w0_ref[...] + lora[:, :width])) - 0.5
        decay = jnp.exp(-jnp.exp(w_log))
        a = jax.nn.sigmoid(a0_ref[...] + lora[:, width:])
        g = jnp.dot(jax.nn.sigmoid(xg).astype(BF16), gup_ref[...], preferred_element_type=F32)
        kk = k * kk_ref[...]
        kk = kk * lax.rsqrt(jnp.maximum(_head_sums(kk * kk, bd_ref), KK_EPS))
        k2 = k * (1.0 + (a - 1.0) * ka_ref[...])
        kb = kk * a
        r_scr[b, 0:tc, :] = r
        w_scr[b, 0:tc, :] = decay
        k_scr[b, 0:tc, :] = k2
        v_scr[b, 0:tc, :] = v
        kn_scr[b, 0:tc, :] = kk
        b_scr[b, 0:tc, :] = kb
        g_scr[b, 0:tc, :] = g
        bonus_scr[b, 0:tc, :] = _head_sums(r * k2 * rk_ref[...], bd_ref) * v

    eye = eye_ref[...]
    eye_hi = eye.astype(BF16)
    eye_lo = eye2_ref[...]

    def bcast(row):
        return jnp.broadcast_to(row, (rows_per_b, width))

    def group(gi, _):
        base = pl.multiple_of(gi * SUBLANES, SUBLANES)
        blk = {name: [scr[b, pl.ds(base, steps), :] for b in range(nb)]
               for name, scr in (("r", r_scr), ("w", w_scr), ("k", k_scr), ("kn", kn_scr), ("b", b_scr))}
        state = [st_scr[b * rows_per_b:(b + 1) * rows_per_b, :] for b in range(nb)]
        nr = rows_per_b
        v_hi, v_lo = [], []
        for b in range(nb):
            v8 = v_scr[b, pl.ds(base, SUBLANES), :]
            hi = v8.astype(BF16).astype(F32)
            lo = v8 - hi
            lane = lax.broadcasted_iota(jnp.int32, lo.shape, 1)
            v_hi.append(hi)
            v_lo.append(jnp.where(lane % RW_HD < RW_HD // 2, pltpu.roll(lo, width - RW_HD // 2, 1),
                                  pltpu.roll(lo, RW_HD // 2, 1)))
        def seg_dot(parts):
            halves = [p[:, c:c + MXU_DIM] for p in parts for c in range(0, width, MXU_DIM)]
            res = jnp.dot(jnp.concatenate(halves, axis=0), bd_ref[...], preferred_element_type=F32)
            nh = width // MXU_DIM
            return [jnp.concatenate([res[(idx * nh + c) * nr:(idx * nh + c + 1) * nr] for c in range(nh)], axis=1)
                    for idx in range(len(parts))]

        def v_column(b, j):
            return (eye_hi * bcast(v_hi[b][j:j + 1, :].astype(BF16))
                    + eye_lo * bcast(v_lo[b][j:j + 1, :].astype(BF16)))

        out_rows = [[] for _ in range(nb)]

        def emit(b, o_col):
            out_rows[b].append(jnp.sum(o_col * eye, axis=0, keepdims=True))

        pending = [[] for _ in range(nb)]
        for j in range(steps):
            row = lambda name, b: bcast(blk[name][b][j:j + 1, :])
            for b in range(nb):
                cols = seg_dot([v_column(b, j)] + pending[b])
                for o_col in cols[1:]:
                    emit(b, o_col)
                s_hi, s_lo = seg_dot(list(_split_bf16(state[b] * row("kn", b))))
                state[b] = state[b] * row("w", b) - (s_hi + s_lo) * row("b", b) + cols[0] * row("k", b)
                pending[b] = [(state[b] * row("r", b)).astype(BF16)]
        for b in range(nb):
            emit(b, seg_dot(pending[b])[0])
        for b in range(nb):
            st_scr[b * rows_per_b:(b + 1) * rows_per_b, :] = state[b]
            oo_scr[b, pl.ds(base, steps), :] = jnp.concatenate(out_rows[b], axis=0)
        return 0

    lax.fori_loop(0, tc // steps, group, 0)

    inv_n = 1.0 / RW_HD
    for b in range(nb):
        o = oo_scr[b, 0:tc, :]
        oc = o - _head_sums(o, bd_ref) * inv_n
        var = _head_sums(oc * oc, bd_ref) * inv_n
        on = oc * lax.rsqrt(var + GN_EPS) * gng_ref[...] + gnb_ref[...] + bonus_scr[b, 0:tc, :]
        o_ref[b] = (on * g_scr[b, 0:tc, :]).astype(o_ref.dtype)

    @pl.when(ci == pl.num_programs(0) - 1)
    def _():
        sout_ref[...] = st_scr[...]


def _rwkv(p_c, shift0, wkv0, mu, w0, w_up, a0, a_up, g_up, k_k, k_a, r_k, gn_g, gn_b, *, tc):
    nb, l, cs = p_c.shape
    width = w0.shape[0]
    heads = width // RW_HD
    s0 = jnp.transpose(wkv0.astype(F32), (0, 2, 1, 3)).reshape(nb * RW_HD, width)
    lora = jnp.zeros((RW_LORA_W + RW_LORA_A, 2 * width), F32)
    lora = lora.at[:RW_LORA_W, :width].set(w_up).at[RW_LORA_W:, width:].set(a_up).astype(BF16)
    lane = jnp.arange(MXU_DIM)
    bd = (lane[:, None] // RW_HD == lane[None, :] // RW_HD).astype(BF16)
    lane_k = jnp.arange(width)[None, :] % RW_HD
    eye = (jnp.arange(RW_HD)[:, None] == lane_k).astype(F32)
    eye2 = ((jnp.arange(RW_HD)[:, None] + RW_HD // 2) % RW_HD == lane_k).astype(BF16)
    row = lambda t: t.reshape(1, -1).astype(F32)
    full = lambda arr: pl.BlockSpec(arr.shape, lambda c: (0,) * arr.ndim)
    small = [shift0.astype(F32), s0, row(mu), row(w0), row(a0), lora, g_up.astype(BF16), row(k_k), row(k_a),
             row(r_k), row(gn_g), row(gn_b), bd, eye, eye2]
    tc_pad = -(-tc // SUBLANES) * SUBLANES
    chunk_scr = pltpu.VMEM((nb, tc_pad, width), F32)
    out, s_out = pl.pallas_call(
        functools.partial(_rwkv_kernel, nb=nb, tc=tc, width=width),
        grid=(l // tc,),
        in_specs=[pl.BlockSpec((nb, tc, cs), lambda c: (0, c, 0))] + [full(t) for t in small],
        out_specs=[pl.BlockSpec((nb, tc, width), lambda c: (0, c, 0)),
                   pl.BlockSpec((nb * RW_HD, width), lambda c: (0, 0))],
        out_shape=[jax.ShapeDtypeStruct((nb, l, width), BF16),
                   jax.ShapeDtypeStruct((nb * RW_HD, width), F32)],
        scratch_shapes=[pltpu.VMEM((nb * RW_HD, width), F32), pltpu.VMEM((nb, cs), F32),
                        pltpu.VMEM((SUBLANES + tc, cs), F32)] + [chunk_scr] * 9,
        compiler_params=_cparams(("arbitrary",), 40),
    )(p_c, *small)
    s_new = jnp.transpose(s_out.reshape(nb, RW_HD, heads, RW_HD), (0, 2, 1, 3))
    return out, s_new


def _conv_kernel(p_ref, c0_ref, w_ref, cb_ref, lng_ref, lnb_ref, o_ref, c1_ref, zc_scr, win_scr, *, tm, taps):
    i = pl.program_id(1)
    width = o_ref.shape[2]
    first = CONV_HALO - (taps - 1)

    @pl.when(i == 0)
    def _():
        zc_scr[0:first, :] = jnp.zeros((first, width), F32)
        zc_scr[first:CONV_HALO, :] = c0_ref[0]

    @pl.when(i > 0)
    def _():
        zc_scr[0:CONV_HALO, :] = zc_scr[tm:tm + CONV_HALO, :]

    p = p_ref[0]
    zc_scr[CONV_HALO:CONV_HALO + tm, :] = p[:, :width] * jax.nn.sigmoid(p[:, width:])
    y = jnp.zeros((tm, width), F32)
    for shift in range(SUBLANES):
        offsets = [o for o in range(first, first + taps) if o % SUBLANES == shift]
        if not offsets:
            continue
        rows = max(offsets) - shift + tm
        win_scr[0:rows, :] = zc_scr[shift:shift + rows, :]
        for o in offsets:
            y = y + win_scr[o - shift:o - shift + tm, :] * w_ref[o - first:o - first + 1, :]
    y = y + cb_ref[...]
    yc = y - jnp.mean(y, axis=-1, keepdims=True)
    var = jnp.mean(yc * yc, axis=-1, keepdims=True)
    yn = yc * lax.rsqrt(var + LN_EPS) * lng_ref[...] + lnb_ref[...]
    o_ref[0] = (yn * jax.nn.sigmoid(yn)).astype(o_ref.dtype)

    @pl.when(i == pl.num_programs(1) - 1)
    def _():
        c1_ref[0] = zc_scr[tm + first:tm + CONV_HALO, :]


def _conv(p_d, conv0, conv_w, conv_b, ln_g, ln_b, *, tm):
    nb, l, w2 = p_d.shape
    width = w2 // 2
    taps = conv_w.shape[0]
    row = lambda t: t.reshape(1, width).astype(F32)
    return pl.pallas_call(
        functools.partial(_conv_kernel, tm=tm, taps=taps),
        grid=(nb, l // tm),
        in_specs=[
            pl.BlockSpec((1, tm, w2), lambda b, i: (b, i, 0)),
            pl.BlockSpec((1, taps - 1, width), lambda b, i: (b, 0, 0)),
            pl.BlockSpec((taps, width), lambda b, i: (0, 0)),
            pl.BlockSpec((1, width), lambda b, i: (0, 0)),
            pl.BlockSpec((1, width), lambda b, i: (0, 0)),
            pl.BlockSpec((1, width), lambda b, i: (0, 0)),
        ],
        out_specs=[pl.BlockSpec((1, tm, width), lambda b, i: (b, i, 0)),
                   pl.BlockSpec((1, taps - 1, width), lambda b, i: (b, 0, 0))],
        out_shape=[jax.ShapeDtypeStruct((nb, l, width), BF16),
                   jax.ShapeDtypeStruct((nb, taps - 1, width), F32)],
        scratch_shapes=[pltpu.VMEM((CONV_HALO + tm, width), F32)] * 2,
        compiler_params=_cparams(("parallel", "arbitrary"), 32),
    )(p_d, conv0.astype(F32), conv_w.astype(F32), row(conv_b), row(ln_g), row(ln_b))


def _merge_kernel(pa_ref, pb_ref, pc_ref, pd_ref, wa_ref, wb_ref, wc_ref, wd_ref,
                  ga_ref, gb_ref, gc_ref, gd_ref, o_ref):
    acc = None
    for p_ref, w_ref, g_ref in ((pa_ref, wa_ref, ga_ref), (pb_ref, wb_ref, gb_ref),
                                (pc_ref, wc_ref, gc_ref), (pd_ref, wd_ref, gd_ref)):
        y = g_ref[...].astype(F32) * jnp.dot(p_ref[...], w_ref[...], preferred_element_type=F32)
        acc = y if acc is None else acc + y
    o_ref[...] = acc.astype(o_ref.dtype)


def _merge(pres, outs, gates, *, tm):
    m, width = pres[0].shape
    d = outs[0].shape[1]
    gate_spec = lambda bidx: pl.BlockSpec((tm, d), lambda i: (i, bidx))
    return pl.pallas_call(
        _merge_kernel,
        grid=(m // tm,),
        in_specs=[pl.BlockSpec((tm, width), lambda i: (i, 0))] * 4
                 + [pl.BlockSpec((width, d), lambda i: (0, 0), pipeline_mode=pl.Buffered(1))] * 4
                 + [gate_spec(bidx) for bidx in range(4)],
        out_specs=pl.BlockSpec((tm, d), lambda i: (i, 0)),
        out_shape=jax.ShapeDtypeStruct((m, d), BF16),
        compiler_params=_cparams(("parallel",), 48),
    )(*pres, *outs, gates, gates, gates, gates)


def _pick_tile(m, pref):
    return pref if m % pref == 0 else m


def _trunk_layer(x, lw, attend, shift0, wkv0, conv0, *, gmlp_chunk, want_v):
    nb, l, d = x.shape
    m = nb * l
    x2 = x.reshape(m, d)
    tm = _pick_tile(m, 1024)
    tm_s = _pick_tile(m, 512)
    w_in, layer = lw["w_in"]
    p_a, q, k, v, p_c, p_d = _in_proj(
        x2, lw["norm_mix"], w_in, layer, 0, lw["b_qn"], lw["b_kn"],
        ("plain", "qnorm", "knorm", "plain", "plain", "plain"), lw["in_widths"][:-1], tm=tm, tn=MXU_DIM)
    (gates,) = _in_proj(x2, lw["norm_mix"], w_in, layer, sum(lw["in_widths"][:-1]), lw["b_qn"], lw["b_kn"],
                        ("sigmoid",), lw["in_widths"][-1:], tm=tm, tn=MXU_DIM)

    width = q.shape[1]
    pre_a, v_a = _gmlp(p_a, lw["a_ln_g"], lw["a_ln_b"], lw["a_mix"], lw["a_bias"],
                       tm=_pick_tile(m, 256), chunk=gmlp_chunk, want_v=want_v)
    pre_b = attend(q.reshape(nb, l, width), k.reshape(nb, l, width), v.reshape(nb, l, width))
    pre_c, wkv1 = _rwkv(p_c.reshape(nb, l, -1), shift0, wkv0, lw["c_mu"], lw["c_w0"], lw["c_w_up"], lw["c_a0"],
                        lw["c_a_up"], lw["c_g_up"], lw["c_k_k"], lw["c_k_a"], lw["c_r_k"], lw["c_gn_g"],
                        lw["c_gn_b"], tc=min(l, 128))
    shift1 = p_c.reshape(nb, l, -1)[:, -1]
    pre_d, conv1 = _conv(p_d.reshape(nb, l, -1), conv0, lw["d_conv_w"], lw["d_conv_b"], lw["d_ln_g"], lw["d_ln_b"],
                         tm=min(l, 256))
    merged = _merge([pre_a, pre_b.reshape(m, width).astype(BF16), pre_c.reshape(m, width), pre_d.reshape(m, width)],
                    [lw["a_out"], lw["b_out"], lw["c_out"], lw["d_out"]], gates, tm=tm_s)
    x2 = _matmul_residual(merged, lw["w_mix_out"], x2, tm=tm_s)
    x2 = _ffn(x2, lw["norm_ffn"], lw["f_gate"], lw["f_up"], lw["f_down"], tm=tm_s, th=512)
    return x2.reshape(nb, l, d), k, v, wkv1, shift1, conv1, v_a


def kernel(x_prompt, x_sample, cache_k, cache_v, page_table, state_wkv, state_shift, state_conv, norm_mix, w_in, a_ln_g, a_ln_b, a_ws, a_bs, a_out, b_qn, b_kn, b_bias, b_out, c_mu, c_w0, c_w_up, c_a0, c_a_up, c_g_up, c_k_k, c_k_a, c_r_k, c_gn_g, c_gn_b, c_out, d_conv_w, d_conv_b, d_ln_g, d_ln_b, d_out, w_mix_out, norm_ffn, f_gate, f_up, f_down):
    depth = w_in.shape[0]
    bp, seq, d_model = x_prompt.shape
    bs, dec = x_sample.shape[:2]
    a_w = a_ln_g.shape[1]
    bw = b_out.shape[1]
    heads = b_bias.shape[1]
    c_shift = c_mu.shape[1]
    d_w = d_ln_g.shape[1]
    chunk = a_ws.shape[2]
    taps = d_conv_w.shape[1]
    gw = a_w // A_GROUPS
    n_phys, page = cache_k.shape[1:3]
    cache_k = cache_k.reshape(depth, n_phys, page * heads, ATT_HD)
    cache_v = cache_v.reshape(depth, n_phys, page * heads, ATT_HD)
    in_widths = (2 * a_w, bw, bw, bw, c_shift, 2 * d_w, 4 * d_model)
    causal = jnp.tril(jnp.ones((chunk, chunk), dtype=bool))
    causal_s = jnp.tril(jnp.ones((dec, dec), dtype=bool))

    yp, ys = x_prompt, x_sample
    outs = [[] for _ in range(11)]
    for l in range(depth):
        lw = {
            "norm_mix": norm_mix[l], "w_in": (w_in, l), "in_widths": in_widths,
            "a_ln_g": a_ln_g[l], "a_ln_b": a_ln_b[l], "a_out": a_out[l].astype(BF16),
            "b_qn": b_qn[l], "b_kn": b_kn[l], "b_out": b_out[l].astype(BF16),
            "c_mu": c_mu[l], "c_w0": c_w0[l], "c_w_up": c_w_up[l], "c_a0": c_a0[l], "c_a_up": c_a_up[l],
            "c_g_up": c_g_up[l], "c_k_k": c_k_k[l], "c_k_a": c_k_a[l], "c_r_k": c_r_k[l],
            "c_gn_g": c_gn_g[l], "c_gn_b": c_gn_b[l], "c_out": c_out[l].astype(BF16),
            "d_conv_w": d_conv_w[l], "d_conv_b": d_conv_b[l], "d_ln_g": d_ln_g[l], "d_ln_b": d_ln_b[l],
            "d_out": d_out[l].astype(BF16),
            "w_mix_out": w_mix_out[l].astype(BF16), "norm_ffn": norm_ffn[l],
            "f_gate": f_gate[l].astype(BF16), "f_up": f_up[l].astype(BF16), "f_down": f_down[l].astype(BF16),
        }
        lw_p = dict(lw)
        lw_p["a_mix"] = jnp.where(causal, a_ws[l], 0.0).astype(BF16)
        lw_p["a_bias"] = jnp.broadcast_to(a_bs[l][:, :, None], (A_GROUPS, chunk, gw)).astype(F32)
        attend_p = functools.partial(_sb_prompt, bias=b_bias[l], tq=1024, tk=256)
        yp, kp, vp, wp, sp, cp, _ = _trunk_layer(
            yp, lw_p, attend_p, jnp.zeros((bp, c_shift), F32), jnp.zeros((bp, c_w0.shape[1] // RW_HD, RW_HD, RW_HD), F32),
            jnp.zeros((bp, taps - 1, d_w), F32), gmlp_chunk=chunk, want_v=False)

        lw_s = dict(lw)
        w_small = jnp.where(causal_s, a_ws[l][:, :dec, :dec], 0.0)
        lw_s["a_mix"] = jnp.einsum("ab,gts->gatbs", jnp.eye(bs, dtype=F32), w_small).reshape(
            A_GROUPS, bs * dec, bs * dec).astype(BF16)
        lw_s["a_bias"] = jnp.broadcast_to(jnp.tile(a_bs[l][:, :dec], (1, bs))[:, :, None],
                                          (A_GROUPS, bs * dec, gw)).astype(F32)
        attend_s = lambda q, k, v: _sb_sample(q, k, v, b_bias[l], cache_k, cache_v, page_table, l, pages_per_step=8)
        ys, ksn, vsn, wsn, ssn, csn, gvs = _trunk_layer(
            ys, lw_s, attend_s, state_shift[l], state_wkv[l], state_conv[l], gmlp_chunk=bs * dec, want_v=True)

        layer_out = (kp.reshape(bp, seq, heads, ATT_HD), vp.reshape(bp, seq, heads, ATT_HD),
                     ksn.reshape(bs, dec, heads, ATT_HD), vsn.reshape(bs, dec, heads, ATT_HD),
                     wp, wsn, sp, ssn, cp, csn, gvs.reshape(bs, dec, a_w))
        for acc, val in zip(outs, layer_out):
            acc.append(val)
    return (yp, ys) + tuple(jnp.stack(o) for o in outs)
```

```python
import functools

import jax
import jax.numpy as jnp
from jax import lax
from jax.experimental import pallas as pl
from jax.experimental.pallas import tpu as pltpu

F32 = jnp.float32
BF16 = jnp.bfloat16

RMS_EPS = 1e-6
LN_EPS = 1e-5
GN_EPS = 64e-5
KK_EPS = 1e-24

LANES = 128
SUBLANES = 8
MXU_DIM = 256
MIB = 1024 * 1024

A_GROUPS = 4
ATT_HD = 128
RW_HD = 64
RW_LORA_W = 64
RW_LORA_A = 64
RW_LORA_G = 128
CONV_HALO = 32


def _cparams(semantics, vmem_mib):
    return pltpu.CompilerParams(dimension_semantics=semantics, vmem_limit_bytes=vmem_mib * MIB)


def _split_bf16(x):
    hi = x.astype(BF16)
    lo = (x - hi.astype(F32)).astype(BF16)
    return hi, lo


def _softplus(y):
    return jnp.maximum(y, 0.0) + jnp.log1p(jnp.exp(-jnp.abs(y)))


def _in_proj_kernel(x_ref, g_ref, w_ref, qg_ref, kg_ref, *rest, segments):
    out_refs = rest[:len(segments)]
    h_scr = rest[len(segments)]
    j = pl.program_id(1)

    @pl.when(j == 0)
    def _():
        x = x_ref[...]
        ms = jnp.mean(x * x, axis=-1, keepdims=True)
        h_scr[...] = (x * lax.rsqrt(ms + RMS_EPS) * g_ref[...]).astype(BF16)

    acc = jnp.dot(h_scr[...], w_ref[...].astype(BF16), preferred_element_type=F32)
    for (kind, start, stop), o_ref in zip(segments, out_refs):
        in_segment = pl.when((j >= start) & (j < stop)) if len(segments) > 1 else (lambda f: f())

        @in_segment
        def _(kind=kind, o_ref=o_ref):
            if kind == "sigmoid":
                o_ref[...] = jax.nn.sigmoid(acc).astype(o_ref.dtype)
            elif kind in ("qnorm", "knorm"):
                hg_ref = qg_ref if kind == "qnorm" else kg_ref
                for c in range(acc.shape[1] // ATT_HD):
                    sl = slice(c * ATT_HD, (c + 1) * ATT_HD)
                    y = acc[:, sl]
                    ms = jnp.mean(y * y, axis=-1, keepdims=True)
                    o_ref[:, sl] = y * lax.rsqrt(ms + RMS_EPS) * hg_ref[...]
            else:
                o_ref[...] = acc


def _in_proj(x, gain, w_all, layer, col0, q_gain, k_gain, seg_kinds, seg_widths, *, tm, tn):
    m, k = x.shape
    n = sum(seg_widths)
    assert col0 % tn == 0
    tile0 = col0 // tn
    segments, out_specs, out_shape = [], [], []
    start = 0
    for kind, width in zip(seg_kinds, seg_widths):
        tiles = width // tn
        assert tiles * tn == width
        segments.append((kind, start, start + tiles))
        out_specs.append(pl.BlockSpec((tm, tn), lambda i, j, s=start, t=tiles: (i, jnp.clip(j - s, 0, t - 1))))
        out_shape.append(jax.ShapeDtypeStruct((m, width), BF16 if kind == "sigmoid" else F32))
        start += tiles
    assert start * tn == n
    return pl.pallas_call(
        functools.partial(_in_proj_kernel, segments=tuple(segments)),
        grid=(m // tm, n // tn),
        in_specs=[
            pl.BlockSpec((tm, k), lambda i, j: (i, 0)),
            pl.BlockSpec((1, k), lambda i, j: (0, 0)),
            pl.BlockSpec((None, k, tn), lambda i, j: (layer, 0, tile0 + j)),
            pl.BlockSpec((1, ATT_HD), lambda i, j: (0, 0)),
            pl.BlockSpec((1, ATT_HD), lambda i, j: (0, 0)),
        ],
        out_specs=out_specs,
        out_shape=out_shape,
        scratch_shapes=[pltpu.VMEM((tm, k), BF16)],
        compiler_params=_cparams(("parallel", "arbitrary"), 48),
    )(x, gain.reshape(1, k), w_all, q_gain.reshape(1, ATT_HD), k_gain.reshape(1, ATT_HD))


def _matmul_res_kernel(a_ref, w_ref, r_ref, o_ref):
    o_ref[...] = r_ref[...] + jnp.dot(a_ref[...], w_ref[...], preferred_element_type=F32)


def _matmul_residual(a, w, res, *, tm):
    m, k = a.shape
    n = w.shape[1]
    return pl.pallas_call(
        _matmul_res_kernel,
        grid=(m // tm,),
        in_specs=[
            pl.BlockSpec((tm, k), lambda i: (i, 0)),
            pl.BlockSpec((k, n), lambda i: (0, 0), pipeline_mode=pl.Buffered(1)),
            pl.BlockSpec((tm, n), lambda i: (i, 0)),
        ],
        out_specs=pl.BlockSpec((tm, n), lambda i: (i, 0)),
        out_shape=jax.ShapeDtypeStruct((m, n), F32),
        compiler_params=_cparams(("parallel",), 40),
    )(a, w, res)


def _ffn_kernel(x_ref, g_ref, wg_ref, wu_ref, wd_ref, o_ref, h_scr, acc_scr):
    j = pl.program_id(1)

    @pl.when(j == 0)
    def _():
        x = x_ref[...]
        ms = jnp.mean(x * x, axis=-1, keepdims=True)
        h_scr[...] = (x * lax.rsqrt(ms + RMS_EPS) * g_ref[...]).astype(BF16)
        acc_scr[...] = jnp.zeros_like(acc_scr)

    h = h_scr[...]
    gate = jnp.dot(h, wg_ref[...], preferred_element_type=F32)
    up = jnp.dot(h, wu_ref[...], preferred_element_type=F32)
    act = (gate * jax.nn.sigmoid(gate) * up).astype(BF16)
    acc_scr[...] += jnp.dot(act, wd_ref[...], preferred_element_type=F32)

    @pl.when(j == pl.num_programs(1) - 1)
    def _():
        o_ref[...] = x_ref[...] + acc_scr[...]


def _ffn(x, gain, wg, wu, wd, *, tm, th):
    m, d = x.shape
    hidden = wg.shape[1]
    return pl.pallas_call(
        _ffn_kernel,
        grid=(m // tm, hidden // th),
        in_specs=[
            pl.BlockSpec((tm, d), lambda i, j: (i, 0)),
            pl.BlockSpec((1, d), lambda i, j: (0, 0)),
            pl.BlockSpec((d, th), lambda i, j: (0, j)),
            pl.BlockSpec((d, th), lambda i, j: (0, j)),
            pl.BlockSpec((th, d), lambda i, j: (j, 0)),
        ],
        out_specs=pl.BlockSpec((tm, d), lambda i, j: (i, 0)),
        out_shape=jax.ShapeDtypeStruct((m, d), F32),
        scratch_shapes=[pltpu.VMEM((tm, d), BF16), pltpu.VMEM((tm, d), F32)],
        compiler_params=_cparams(("parallel", "arbitrary"), 52),
    )(x, gain.reshape(1, d), wg, wu, wd)


def _gmlp_kernel(p_ref, lng_ref, lnb_ref, wm_ref, bs_ref, o_ref, *v_out, chunk):
    z = jax.nn.gelu(p_ref[...])
    width = z.shape[1] // 2
    u = z[:, :width]
    v = z[:, width:]
    vc = v - jnp.mean(v, axis=-1, keepdims=True)
    var = jnp.mean(vc * vc, axis=-1, keepdims=True)
    v = vc * lax.rsqrt(var + LN_EPS) * lng_ref[...] + lnb_ref[...]
    if v_out:
        v_out[0][...] = v
    gw = width // A_GROUPS
    for ci in range(z.shape[0] // chunk):
        rows = slice(ci * chunk, (ci + 1) * chunk)
        for g in range(A_GROUPS):
            cols = slice(g * gw, (g + 1) * gw)
            s = jnp.dot(wm_ref[g], v[rows, cols].astype(BF16), preferred_element_type=F32) + bs_ref[g]
            o_ref[rows, cols] = (u[rows, cols] * s).astype(o_ref.dtype)


def _gmlp(p_a, ln_g, ln_b, w_mix, b_mix, *, tm, chunk, want_v):
    m, w2 = p_a.shape
    width = w2 // 2
    out_shape = [jax.ShapeDtypeStruct((m, width), BF16)]
    out_specs = [pl.BlockSpec((tm, width), lambda i: (i, 0))]
    if want_v:
        out_shape.append(jax.ShapeDtypeStruct((m, width), F32))
        out_specs.append(pl.BlockSpec((tm, width), lambda i: (i, 0)))
    res = pl.pallas_call(
        functools.partial(_gmlp_kernel, chunk=chunk),
        grid=(m // tm,),
        in_specs=[
            pl.BlockSpec((tm, w2), lambda i: (i, 0)),
            pl.BlockSpec((1, width), lambda i: (0, 0)),
            pl.BlockSpec((1, width), lambda i: (0, 0)),
            pl.BlockSpec(w_mix.shape, lambda i: (0, 0, 0)),
            pl.BlockSpec(b_mix.shape, lambda i: (0, 0, 0)),
        ],
        out_specs=out_specs,
        out_shape=out_shape,
        compiler_params=_cparams(("parallel",), 32),
    )(p_a, ln_g.reshape(1, width), ln_b.reshape(1, width), w_mix, b_mix)
    return res if want_v else (res[0], None)


def _stick_terms(z):
    t = jnp.log(1.0 + jnp.exp(-jnp.abs(z)))
    return jnp.minimum(z, 0.0) - t, jnp.minimum(-z, 0.0) - t


def _suffix_sums(l1m, tri_ref):
    hi, lo = _split_bf16(l1m)
    n = l1m.shape[0]
    both = jnp.dot(jnp.concatenate([hi, lo], axis=0), tri_ref[...], preferred_element_type=F32)
    return both[:n] + both[n:]


def _sb_prompt_kernel(bias_ref, q_ref, k_ref, v_ref, tri_ref, o_ref, *, tq, tk, heads, scale):
    i = pl.program_id(1)
    band = tq // tk
    cols = [slice(h * ATT_HD, (h + 1) * ATT_HD) for h in range(heads)]
    q = [q_ref[0, :, c].astype(BF16) for c in cols]

    def visit(j, h, carry, acc, first_row=None):
        start = pl.multiple_of(j * tk, tk)
        k_t = k_ref[0, pl.ds(start, tk), cols[h]].astype(BF16)
        v_t = v_ref[0, pl.ds(start, tk), cols[h]].astype(BF16)
        r0 = first_row or 0
        s = lax.dot_general(q[h][r0:], k_t, (((1,), (1,)), ((), ())), preferred_element_type=F32)
        ls, l1m = _stick_terms(s * scale + bias_ref[h])
        if first_row is not None:
            row = lax.broadcasted_iota(jnp.int32, (tq - r0, tk), 0)
            col = lax.broadcasted_iota(jnp.int32, (tq - r0, tk), 1)
            vis = col < row
            l1m = jnp.where(vis, l1m, 0.0)
        att = jnp.exp(ls + _suffix_sums(l1m, tri_ref) + carry[r0:])
        if first_row is not None:
            att = jnp.where(vis, att, 0.0)
        carry_new = carry[r0:] + jnp.sum(l1m, axis=1, keepdims=True)
        acc_new = acc[r0:] + jnp.dot(att.astype(BF16), v_t, preferred_element_type=F32)
        if r0:
            carry_new = jnp.concatenate([carry[:r0], carry_new], axis=0)
            acc_new = jnp.concatenate([acc[:r0], acc_new], axis=0)
        return carry_new, acc_new

    state = [jnp.zeros((tq, 1), F32), jnp.zeros((tq, ATT_HD), F32)] * heads
    for t in reversed(range(band)):
        for h in range(heads):
            state[2 * h], state[2 * h + 1] = visit(i * band + t, h, state[2 * h], state[2 * h + 1],
                                                   first_row=t * tk)

    def body(jj, state):
        new = []
        for h in range(heads):
            new.extend(visit(i * band - 1 - jj, h, state[2 * h], state[2 * h + 1]))
        return tuple(new)

    state = lax.fori_loop(0, i * band, body, tuple(state))
    for h in range(heads):
        o_ref[0, :, cols[h]] = state[2 * h + 1].astype(o_ref.dtype)


def _strict_upper(n):
    r = lax.broadcasted_iota(jnp.int32, (n, n), 0)
    c = lax.broadcasted_iota(jnp.int32, (n, n), 1)
    return (r > c).astype(BF16)


def _sb_prompt(q, k, v, bias, *, tq, tk):
    b, l, w = q.shape
    heads = w // ATT_HD
    assert tq % tk == 0 and l % tq == 0
    return pl.pallas_call(
        functools.partial(_sb_prompt_kernel, tq=tq, tk=tk, heads=heads, scale=ATT_HD ** -0.5),
        grid=(b, l // tq),
        in_specs=[
            pl.BlockSpec(memory_space=pltpu.SMEM),
            pl.BlockSpec((1, tq, w), lambda bi, i: (bi, i, 0)),
            pl.BlockSpec((1, l, w), lambda bi, i: (bi, 0, 0), pipeline_mode=pl.Buffered(1)),
            pl.BlockSpec((1, l, w), lambda bi, i: (bi, 0, 0), pipeline_mode=pl.Buffered(1)),
            pl.BlockSpec((tk, tk), lambda bi, i: (0, 0)),
        ],
        out_specs=pl.BlockSpec((1, tq, w), lambda bi, i: (bi, i, 0)),
        out_shape=jax.ShapeDtypeStruct((b, l, w), BF16),
        compiler_params=_cparams(("parallel", "arbitrary"), 48),
    )(bias, q, k, v, _strict_upper(tk))


def _sb_sample_kernel(pt_ref, bias_ref, qrows_ref, knew_ref, vnew_ref, *rest, pages_per_step, heads, n_q, scale):
    k_refs = rest[:pages_per_step]
    v_refs = rest[pages_per_step:2 * pages_per_step]
    tri_ref, o_ref, carry_scr, acc_scr = rest[2 * pages_per_step:]
    s = pl.program_id(1)
    page = tri_ref.shape[0]
    n_rows = heads * n_q
    q = qrows_ref[0].astype(BF16)

    def block(k_blk, v_blk, masked):
        n_blk = k_blk.shape[0] // page
        sc = lax.dot_general(q, k_blk.astype(BF16), (((1,), (1,)), ((), ())), preferred_element_type=F32)
        z = sc * scale + bias_ref[:, 0:n_blk * page]
        ls, l1m = _stick_terms(z)
        if masked:
            row = lax.broadcasted_iota(jnp.int32, (n_rows, page), 0)
            col = lax.broadcasted_iota(jnp.int32, (n_rows, page), 1)
            vis = col < (row % n_q)
            l1m = jnp.where(vis, l1m, 0.0)
        pages = [slice(p * page, (p + 1) * page) for p in range(n_blk)]
        local = _suffix_sums(jnp.concatenate([l1m[:, sl] for sl in pages], axis=0), tri_ref)
        carry = carry_scr[...]
        between = []
        for p, sl in enumerate(pages):
            between.append(local[p * n_rows:(p + 1) * n_rows] + carry)
            carry = carry + jnp.sum(l1m[:, sl], axis=1, keepdims=True)
        att = jnp.exp(ls + jnp.concatenate(between, axis=1))
        if masked:
            att = jnp.where(vis, att, 0.0)
        acc_scr[...] += jnp.dot(att.astype(BF16), v_blk.astype(BF16), preferred_element_type=F32)
        carry_scr[...] = carry

    @pl.when(s == 0)
    def _():
        carry_scr[...] = jnp.zeros_like(carry_scr)
        acc_scr[...] = jnp.zeros_like(acc_scr)
        block(knew_ref[0], vnew_ref[0], True)

    def heads_to_lanes(ref):
        return jnp.concatenate([ref[pl.ds(h, page, stride=heads), :] for h in range(heads)], axis=1)

    block(jnp.concatenate([heads_to_lanes(r) for r in k_refs], axis=0),
          jnp.concatenate([heads_to_lanes(r) for r in v_refs], axis=0), False)

    @pl.when(s == pl.num_programs(1) - 1)
    def _():
        acc = acc_scr[...]
        for h in range(heads):
            o_ref[0, :, h * ATT_HD:(h + 1) * ATT_HD] = acc[h * n_q:(h + 1) * n_q, h * ATT_HD:(h + 1) * ATT_HD]


def _sb_sample(q, k_new, v_new, bias, cache_k, cache_v, page_table, layer, *, pages_per_step):
    b, n_q, w = q.shape
    heads = w // ATT_HD
    page = cache_k.shape[2] // heads
    n_pages = page_table.shape[1]
    n_rows = heads * n_q
    eye = jnp.eye(heads, dtype=F32)
    qrows = jnp.einsum("bthd,hg->bhtgd", q.reshape(b, n_q, heads, ATT_HD), eye).reshape(b, n_rows, w)
    pad = ((0, 0), (0, page - n_q), (0, 0))
    k_pad = jnp.pad(k_new, pad)
    v_pad = jnp.pad(v_new, pad)
    bias_rows = jnp.broadcast_to(jnp.repeat(bias, n_q)[:, None], (n_rows, page * pages_per_step))

    def page_spec(p):
        def imap(bi, s, pt):
            return (layer, pt[bi, n_pages - 1 - (s * pages_per_step + p)], 0, 0)
        return pl.BlockSpec((None, None, page * heads, ATT_HD), imap)

    grid_spec = pltpu.PrefetchScalarGridSpec(
        num_scalar_prefetch=1,
        grid=(b, n_pages // pages_per_step),
        in_specs=[
            pl.BlockSpec((n_rows, page * pages_per_step), lambda bi, s, pt: (0, 0)),
            pl.BlockSpec((1, n_rows, w), lambda bi, s, pt: (bi, 0, 0)),
            pl.BlockSpec((1, page, w), lambda bi, s, pt: (bi, 0, 0)),
            pl.BlockSpec((1, page, w), lambda bi, s, pt: (bi, 0, 0)),
        ] + [page_spec(p) for p in range(pages_per_step)] * 2 + [
            pl.BlockSpec((page, page), lambda bi, s, pt: (0, 0)),
        ],
        out_specs=pl.BlockSpec((1, n_q, w), lambda bi, s, pt: (bi, 0, 0)),
        scratch_shapes=[pltpu.VMEM((n_rows, 1), F32), pltpu.VMEM((n_rows, w), F32)],
    )
    return pl.pallas_call(
        functools.partial(_sb_sample_kernel, pages_per_step=pages_per_step, heads=heads, n_q=n_q,
                          scale=ATT_HD ** -0.5),
        grid_spec=grid_spec,
        out_shape=jax.ShapeDtypeStruct((b, n_q, w), F32),
        compiler_params=_cparams(("parallel", "arbitrary"), 40),
    )(page_table, bias_rows, qrows, k_pad, v_pad,
      *([cache_k] * pages_per_step), *([cache_v] * pages_per_step), _strict_upper(page))


def _head_sums(x, bd_ref):
    hi, lo = _split_bf16(x)
    return _bd_dot(hi, bd_ref) + _bd_dot(lo, bd_ref)


def _bd_dot(lhs, bd_ref):
    bd = bd_ref[...]
    outs = [jnp.dot(lhs[:, c:c + MXU_DIM], bd, preferred_element_type=F32)
            for c in range(0, lhs.shape[1], MXU_DIM)]
    return jnp.concatenate(outs, axis=1)


def _rwkv_kernel(p_ref, shift0_ref, s0_ref, mu_ref, w0_ref, a0_ref, lora_ref, gup_ref, kk_ref, ka_ref, rk_ref,
                 gng_ref, gnb_ref, bd_ref, eye_ref, eye2_ref,
                 o_ref, sout_ref,
                 st_scr, carry_scr, xs_scr, kn_scr, r_scr, w_scr, k_scr, v_scr, b_scr, g_scr, bonus_scr, oo_scr,
                 *, nb, tc, width):
    ci = pl.program_id(0)
    rows_per_b = RW_HD
    steps = min(tc, SUBLANES)

    @pl.when(ci == 0)
    def _():
        st_scr[...] = s0_ref[...]
        carry_scr[...] = shift0_ref[...]
        xs_scr[...] = jnp.zeros_like(xs_scr)
        if tc % SUBLANES:
            v_scr[...] = jnp.zeros_like(v_scr)

    for b in range(nb):
        pc = p_ref[b]
        xs_scr[SUBLANES:SUBLANES + tc, :] = pc
        xs_scr[SUBLANES - 1:SUBLANES, :] = carry_scr[b:b + 1, :]
        prev = xs_scr[SUBLANES - 1:SUBLANES - 1 + tc, :]
        carry_scr[b:b + 1, :] = pc[tc - 1:tc, :]
        xs = pc + (prev - pc) * mu_ref[...]
        r = xs[:, 0:width]
        k = xs[:, width:2 * width]
        v = xs[:, 2 * width:3 * width]
        xwa = xs[:, 3 * width:3 * width + RW_LORA_W + RW_LORA_A]
        xg = xs[:, 3 * width + RW_LORA_W + RW_LORA_A:]
        lane = lax.broadcasted_iota(jnp.int32, xwa.shape, 1)
        lora_in = jnp.where(lane < RW_LORA_W, jnp.tanh(xwa), xwa).astype(BF16)
        lora = jnp.dot(lora_in, lora_ref[...], preferred_element_type=F32)
        w_log = -_softplus(-(w0_ref[...] + lora[:, :width])) - 0.5
        decay = jnp.exp(-jnp.exp(w_log))
        a = jax.nn.sigmoid(a0_ref[...] + lora[:, width:])
        g = jnp.dot(jax.nn.sigmoid(xg).astype(BF16), gup_ref[...], preferred_element_type=F32)
        kk = k * kk_ref[...]
        kk = kk * lax.rsqrt(jnp.maximum(_head_sums(kk * kk, bd_ref), KK_EPS))
        k2 = k * (1.0 + (a - 1.0) * ka_ref[...])
        kb = kk * a
        r_scr[b, 0:tc, :] = r
        w_scr[b, 0:tc, :] = decay
        k_scr[b, 0:tc, :] = k2
        v_scr[b, 0:tc, :] = v
        kn_scr[b, 0:tc, :] = kk
        b_scr[b, 0:tc, :] = kb
        g_scr[b, 0:tc, :] = g
        bonus_scr[b, 0:tc, :] = _head_sums(r * k2 * rk_ref[...], bd_ref) * v

    eye = eye_ref[...]
    eye_hi = eye.astype(BF16)
    eye_lo = eye2_ref[...]

    def bcast(row):
        return jnp.broadcast_to(row, (rows_per_b, width))

    def group(gi, _):
        base = pl.multiple_of(gi * SUBLANES, SUBLANES)
        blk = {name: [scr[b, pl.ds(base, steps), :] for b in range(nb)]
               for name, scr in (("r", r_scr), ("w", w_scr), ("k", k_scr), ("kn", kn_scr), ("b", b_scr))}
        state = [st_scr[b * rows_per_b:(b + 1) * rows_per_b, :] for b in range(nb)]
        nr = rows_per_b
        v_hi, v_lo = [], []
        for b in range(nb):
            v8 = v_scr[b, pl.ds(base, SUBLANES), :]
            hi = v8.astype(BF16).astype(F32)
            lo = v8 - hi
            lane = lax.broadcasted_iota(jnp.int32, lo.shape, 1)
            v_hi.append(hi)
            v_lo.append(jnp.where(lane % RW_HD < RW_HD // 2, pltpu.roll(lo, width - RW_HD // 2, 1),
                                  pltpu.roll(lo, RW_HD // 2, 1)))
        def seg_dot(parts):
            halves = [p[:, c:c + MXU_DIM] for p in parts for c in range(0, width, MXU_DIM)]
            res = jnp.dot(jnp.concatenate(halves, axis=0), bd_ref[...], preferred_element_type=F32)
            nh = width // MXU_DIM
            return [jnp.concatenate([res[(idx * nh + c) * nr:(idx * nh + c + 1) * nr] for c in range(nh)], axis=1)
                    for idx in range(len(parts))]

        def v_column(b, j):
            return (eye_hi * bcast(v_hi[b][j:j + 1, :].astype(BF16))
                    + eye_lo * bcast(v_lo[b][j:j + 1, :].astype(BF16)))

        out_rows = [[] for _ in range(nb)]

        def emit(b, o_col):
            out_rows[b].append(jnp.sum(o_col * eye, axis=0, keepdims=True))

        pending = [[] for _ in range(nb)]
        for j in range(steps):
            row = lambda name, b: bcast(blk[name][b][j:j + 1, :])
            for b in range(nb):
                cols = seg_dot([v_column(b, j)] + pending[b])
                for o_col in cols[1:]:
                    emit(b, o_col)
                (s_kk,) = seg_dot([(state[b] * row("kn", b)).astype(BF16)])
                state[b] = state[b] * row("w", b) - s_kk * row("b", b) + cols[0] * row("k", b)
                pending[b] = [(state[b] * row("r", b)).astype(BF16)]
        for b in range(nb):
            emit(b, seg_dot(pending[b])[0])
        for b in range(nb):
            st_scr[b * rows_per_b:(b + 1) * rows_per_b, :] = state[b]
            oo_scr[b, pl.ds(base, steps), :] = jnp.concatenate(out_rows[b], axis=0)
        return 0

    lax.fori_loop(0, tc // steps, group, 0)

    inv_n = 1.0 / RW_HD
    for b in range(nb):
        o = oo_scr[b, 0:tc, :]
        oc = o - _head_sums(o, bd_ref) * inv_n
        var = _head_sums(oc * oc, bd_ref) * inv_n
        on = oc * lax.rsqrt(var + GN_EPS) * gng_ref[...] + gnb_ref[...] + bonus_scr[b, 0:tc, :]
        o_ref[b] = (on * g_scr[b, 0:tc, :]).astype(o_ref.dtype)

    @pl.when(ci == pl.num_programs(0) - 1)
    def _():
        sout_ref[...] = st_scr[...]


def _rwkv(p_c, shift0, wkv0, mu, w0, w_up, a0, a_up, g_up, k_k, k_a, r_k, gn_g, gn_b, *, tc):
    nb, l, cs = p_c.shape
    width = w0.shape[0]
    heads = width // RW_HD
    s0 = jnp.transpose(wkv0.astype(F32), (0, 2, 1, 3)).reshape(nb * RW_HD, width)
    lora = jnp.zeros((RW_LORA_W + RW_LORA_A, 2 * width), F32)
    lora = lora.at[:RW_LORA_W, :width].set(w_up).at[RW_LORA_W:, width:].set(a_up).astype(BF16)
    lane = jnp.arange(MXU_DIM)
    bd = (lane[:, None] // RW_HD == lane[None, :] // RW_HD).astype(BF16)
    lane_k = jnp.arange(width)[None, :] % RW_HD
    eye = (jnp.arange(RW_HD)[:, None] == lane_k).astype(F32)
    eye2 = ((jnp.arange(RW_HD)[:, None] + RW_HD // 2) % RW_HD == lane_k).astype(BF16)
    row = lambda t: t.reshape(1, -1).astype(F32)
    full = lambda arr: pl.BlockSpec(arr.shape, lambda c: (0,) * arr.ndim)
    small = [shift0.astype(F32), s0, row(mu), row(w0), row(a0), lora, g_up.astype(BF16), row(k_k), row(k_a),
             row(r_k), row(gn_g), row(gn_b), bd, eye, eye2]
    tc_pad = -(-tc // SUBLANES) * SUBLANES
    chunk_scr = pltpu.VMEM((nb, tc_pad, width), F32)
    out, s_out = pl.pallas_call(
        functools.partial(_rwkv_kernel, nb=nb, tc=tc, width=width),
        grid=(l // tc,),
        in_specs=[pl.BlockSpec((nb, tc, cs), lambda c: (0, c, 0))] + [full(t) for t in small],
        out_specs=[pl.BlockSpec((nb, tc, width), lambda c: (0, c, 0)),
                   pl.BlockSpec((nb * RW_HD, width), lambda c: (0, 0))],
        out_shape=[jax.ShapeDtypeStruct((nb, l, width), BF16),
                   jax.ShapeDtypeStruct((nb * RW_HD, width), F32)],
        scratch_shapes=[pltpu.VMEM((nb * RW_HD, width), F32), pltpu.VMEM((nb, cs), F32),
                        pltpu.VMEM((SUBLANES + tc, cs), F32)] + [chunk_scr] * 9,
        compiler_params=_cparams(("arbitrary",), 40),
    )(p_c, *small)
    s_new = jnp.transpose(s_out.reshape(nb, RW_HD, heads, RW_HD), (0, 2, 1, 3))
    return out, s_new


def _conv_kernel(p_ref, c0_ref, w_ref, cb_ref, lng_ref, lnb_ref, o_ref, c1_ref, zc_scr, win_scr, *, tm, taps):
    i = pl.program_id(1)
    width = o_ref.shape[2]
    first = CONV_HALO - (taps - 1)

    @pl.when(i == 0)
    def _():
        zc_scr[0:first, :] = jnp.zeros((first, width), F32)
        zc_scr[first:CONV_HALO, :] = c0_ref[0]

    @pl.when(i > 0)
    def _():
        zc_scr[0:CONV_HALO, :] = zc_scr[tm:tm + CONV_HALO, :]

    p = p_ref[0]
    zc_scr[CONV_HALO:CONV_HALO + tm, :] = p[:, :width] * jax.nn.sigmoid(p[:, width:])
    y = jnp.zeros((tm, width), F32)
    for shift in range(SUBLANES):
        offsets = [o for o in range(first, first + taps) if o % SUBLANES == shift]
        if not offsets:
            continue
        rows = max(offsets) - shift + tm
        win_scr[0:rows, :] = zc_scr[shift:shift + rows, :]
        for o in offsets:
            y = y + win_scr[o - shift:o - shift + tm, :] * w_ref[o - first:o - first + 1, :]
    y = y + cb_ref[...]
    yc = y - jnp.mean(y, axis=-1, keepdims=True)
    var = jnp.mean(yc * yc, axis=-1, keepdims=True)
    yn = yc * lax.rsqrt(var + LN_EPS) * lng_ref[...] + lnb_ref[...]
    o_ref[0] = (yn * jax.nn.sigmoid(yn)).astype(o_ref.dtype)

    @pl.when(i == pl.num_programs(1) - 1)
    def _():
        c1_ref[0] = zc_scr[tm + first:tm + CONV_HALO, :]


def _conv(p_d, conv0, conv_w, conv_b, ln_g, ln_b, *, tm):
    nb, l, w2 = p_d.shape
    width = w2 // 2
    taps = conv_w.shape[0]
    row = lambda t: t.reshape(1, width).astype(F32)
    return pl.pallas_call(
        functools.partial(_conv_kernel, tm=tm, taps=taps),
        grid=(nb, l // tm),
        in_specs=[
            pl.BlockSpec((1, tm, w2), lambda b, i: (b, i, 0)),
            pl.BlockSpec((1, taps - 1, width), lambda b, i: (b, 0, 0)),
            pl.BlockSpec((taps, width), lambda b, i: (0, 0)),
            pl.BlockSpec((1, width), lambda b, i: (0, 0)),
            pl.BlockSpec((1, width), lambda b, i: (0, 0)),
            pl.BlockSpec((1, width), lambda b, i: (0, 0)),
        ],
        out_specs=[pl.BlockSpec((1, tm, width), lambda b, i: (b, i, 0)),
                   pl.BlockSpec((1, taps - 1, width), lambda b, i: (b, 0, 0))],
        out_shape=[jax.ShapeDtypeStruct((nb, l, width), BF16),
                   jax.ShapeDtypeStruct((nb, taps - 1, width), F32)],
        scratch_shapes=[pltpu.VMEM((CONV_HALO + tm, width), F32)] * 2,
        compiler_params=_cparams(("parallel", "arbitrary"), 32),
    )(p_d, conv0.astype(F32), conv_w.astype(F32), row(conv_b), row(ln_g), row(ln_b))


def _merge_kernel(pa_ref, pb_ref, pc_ref, pd_ref, wa_ref, wb_ref, wc_ref, wd_ref,
                  ga_ref, gb_ref, gc_ref, gd_ref, o_ref):
    acc = None
    for p_ref, w_ref, g_ref in ((pa_ref, wa_ref, ga_ref), (pb_ref, wb_ref, gb_ref),
                                (pc_ref, wc_ref, gc_ref), (pd_ref, wd_ref, gd_ref)):
        y = g_ref[...].astype(F32) * jnp.dot(p_ref[...], w_ref[...], preferred_element_type=F32)
        acc = y if acc is None else acc + y
    o_ref[...] = acc.astype(o_ref.dtype)


def _merge(pres, outs, gates, *, tm):
    m, width = pres[0].shape
    d = outs[0].shape[1]
    gate_spec = lambda bidx: pl.BlockSpec((tm, d), lambda i: (i, bidx))
    return pl.pallas_call(
        _merge_kernel,
        grid=(m // tm,),
        in_specs=[pl.BlockSpec((tm, width), lambda i: (i, 0))] * 4
                 + [pl.BlockSpec((width, d), lambda i: (0, 0), pipeline_mode=pl.Buffered(1))] * 4
                 + [gate_spec(bidx) for bidx in range(4)],
        out_specs=pl.BlockSpec((tm, d), lambda i: (i, 0)),
        out_shape=jax.ShapeDtypeStruct((m, d), BF16),
        compiler_params=_cparams(("parallel",), 48),
    )(*pres, *outs, gates, gates, gates, gates)


def _pick_tile(m, pref):
    return pref if m % pref == 0 else m


def _trunk_layer(x, lw, attend, shift0, wkv0, conv0, *, gmlp_chunk, want_v):
    nb, l, d = x.shape
    m = nb * l
    x2 = x.reshape(m, d)
    tm = _pick_tile(m, 1024)
    tm_s = _pick_tile(m, 512)
    w_in, layer = lw["w_in"]
    p_a, q, k, v, p_c, p_d = _in_proj(
        x2, lw["norm_mix"], w_in, layer, 0, lw["b_qn"], lw["b_kn"],
        ("plain", "qnorm", "knorm", "plain", "plain", "plain"), lw["in_widths"][:-1], tm=tm, tn=MXU_DIM)
    (gates,) = _in_proj(x2, lw["norm_mix"], lw["w_gate"], 0, 0, lw["b_qn"], lw["b_kn"],
                        ("sigmoid",), lw["in_widths"][-1:], tm=tm, tn=2 * MXU_DIM)

    width = q.shape[1]
    pre_a, v_a = _gmlp(p_a, lw["a_ln_g"], lw["a_ln_b"], lw["a_mix"], lw["a_bias"],
                       tm=_pick_tile(m, 256), chunk=gmlp_chunk, want_v=want_v)
    pre_b = attend(q.reshape(nb, l, width), k.reshape(nb, l, width), v.reshape(nb, l, width))
    pre_c, wkv1 = _rwkv(p_c.reshape(nb, l, -1), shift0, wkv0, lw["c_mu"], lw["c_w0"], lw["c_w_up"], lw["c_a0"],
                        lw["c_a_up"], lw["c_g_up"], lw["c_k_k"], lw["c_k_a"], lw["c_r_k"], lw["c_gn_g"],
                        lw["c_gn_b"], tc=min(l, 128))
    shift1 = p_c.reshape(nb, l, -1)[:, -1]
    pre_d, conv1 = _conv(p_d.reshape(nb, l, -1), conv0, lw["d_conv_w"], lw["d_conv_b"], lw["d_ln_g"], lw["d_ln_b"],
                         tm=min(l, 256))
    merged = _merge([pre_a, pre_b.reshape(m, width).astype(BF16), pre_c.reshape(m, width), pre_d.reshape(m, width)],
                    [lw["a_out"], lw["b_out"], lw["c_out"], lw["d_out"]], gates, tm=tm_s)
    x2 = _matmul_residual(merged, lw["w_mix_out"], x2, tm=tm_s)
    x2 = _ffn(x2, lw["norm_ffn"], lw["f_gate"], lw["f_up"], lw["f_down"], tm=tm_s, th=512)
    return x2.reshape(nb, l, d), k, v, wkv1, shift1, conv1, v_a


def kernel(x_prompt, x_sample, cache_k, cache_v, page_table, state_wkv, state_shift, state_conv, norm_mix, w_in, a_ln_g, a_ln_b, a_ws, a_bs, a_out, b_qn, b_kn, b_bias, b_out, c_mu, c_w0, c_w_up, c_a0, c_a_up, c_g_up, c_k_k, c_k_a, c_r_k, c_gn_g, c_gn_b, c_out, d_conv_w, d_conv_b, d_ln_g, d_ln_b, d_out, w_mix_out, norm_ffn, f_gate, f_up, f_down):
    depth = w_in.shape[0]
    bp, seq, d_model = x_prompt.shape
    bs, dec = x_sample.shape[:2]
    a_w = a_ln_g.shape[1]
    bw = b_out.shape[1]
    heads = b_bias.shape[1]
    c_shift = c_mu.shape[1]
    d_w = d_ln_g.shape[1]
    chunk = a_ws.shape[2]
    taps = d_conv_w.shape[1]
    gw = a_w // A_GROUPS
    n_phys, page = cache_k.shape[1:3]
    cache_k = cache_k.reshape(depth, n_phys, page * heads, ATT_HD)
    cache_v = cache_v.reshape(depth, n_phys, page * heads, ATT_HD)
    in_widths = (2 * a_w, bw, bw, bw, c_shift, 2 * d_w, 4 * d_model)
    causal = jnp.tril(jnp.ones((chunk, chunk), dtype=bool))
    causal_s = jnp.tril(jnp.ones((dec, dec), dtype=bool))

    yp, ys = x_prompt, x_sample
    outs = [[] for _ in range(11)]
    for l in range(depth):
        lw = {
            "norm_mix": norm_mix[l], "w_in": (w_in, l), "in_widths": in_widths,
            "w_gate": w_in[l][None, :, -in_widths[-1]:].astype(BF16),
            "a_ln_g": a_ln_g[l], "a_ln_b": a_ln_b[l], "a_out": a_out[l].astype(BF16),
            "b_qn": b_qn[l], "b_kn": b_kn[l], "b_out": b_out[l].astype(BF16),
            "c_mu": c_mu[l], "c_w0": c_w0[l], "c_w_up": c_w_up[l], "c_a0": c_a0[l], "c_a_up": c_a_up[l],
            "c_g_up": c_g_up[l], "c_k_k": c_k_k[l], "c_k_a": c_k_a[l], "c_r_k": c_r_k[l],
            "c_gn_g": c_gn_g[l], "c_gn_b": c_gn_b[l], "c_out": c_out[l].astype(BF16),
            "d_conv_w": d_conv_w[l], "d_conv_b": d_conv_b[l], "d_ln_g": d_ln_g[l], "d_ln_b": d_ln_b[l],
            "d_out": d_out[l].astype(BF16),
            "w_mix_out": w_mix_out[l].astype(BF16), "norm_ffn": norm_ffn[l],
            "f_gate": f_gate[l].astype(BF16), "f_up": f_up[l].astype(BF16), "f_down": f_down[l].astype(BF16),
        }
        lw_p = dict(lw)
        lw_p["a_mix"] = jnp.where(causal, a_ws[l], 0.0).astype(BF16)
        lw_p["a_bias"] = jnp.broadcast_to(a_bs[l][:, :, None], (A_GROUPS, chunk, gw)).astype(F32)
        attend_p = functools.partial(_sb_prompt, bias=b_bias[l], tq=1024, tk=256)
        yp, kp, vp, wp, sp, cp, _ = _trunk_layer(
            yp, lw_p, attend_p, jnp.zeros((bp, c_shift), F32), jnp.zeros((bp, c_w0.shape[1] // RW_HD, RW_HD, RW_HD), F32),
            jnp.zeros((bp, taps - 1, d_w), F32), gmlp_chunk=chunk, want_v=False)

        lw_s = dict(lw)
        w_small = jnp.where(causal_s, a_ws[l][:, :dec, :dec], 0.0)
        lw_s["a_mix"] = jnp.einsum("ab,gts->gatbs", jnp.eye(bs, dtype=F32), w_small).reshape(
            A_GROUPS, bs * dec, bs * dec).astype(BF16)
        lw_s["a_bias"] = jnp.broadcast_to(jnp.tile(a_bs[l][:, :dec], (1, bs))[:, :, None],
                                          (A_GROUPS, bs * dec, gw)).astype(F32)
        attend_s = lambda q, k, v: _sb_sample(q, k, v, b_bias[l], cache_k, cache_v, page_table, l, pages_per_step=8)
        ys, ksn, vsn, wsn, ssn, csn, gvs = _trunk_layer(
            ys, lw_s, attend_s, state_shift[l], state_wkv[l], state_conv[l], gmlp_chunk=bs * dec, want_v=True)

        layer_out = (kp.reshape(bp, seq, heads, ATT_HD), vp.reshape(bp, seq, heads, ATT_HD),
                     ksn.reshape(bs, dec, heads, ATT_HD), vsn.reshape(bs, dec, heads, ATT_HD),
                     wp, wsn, sp, ssn, cp, csn, gvs.reshape(bs, dec, a_w))
        for acc, val in zip(outs, layer_out):
            acc.append(val)
    return (yp, ys) + tuple(jnp.stack(o) for o in outs)
```

```python
import functools

import jax
import jax.numpy as jnp
from jax import lax
from jax.experimental import pallas as pl
from jax.experimental.pallas import tpu as pltpu

F32 = jnp.float32
BF16 = jnp.bfloat16

RMS_EPS = 1e-6
LN_EPS = 1e-5
GN_EPS = 64e-5
KK_EPS = 1e-24

LANES = 128
SUBLANES = 8
MXU_DIM = 256
MIB = 1024 * 1024

A_GROUPS = 4
ATT_HD = 128
RW_HD = 64
RW_LORA_W = 64
RW_LORA_A = 64
RW_LORA_G = 128
CONV_HALO = 32


def _cparams(semantics, vmem_mib):
    return pltpu.CompilerParams(dimension_semantics=semantics, vmem_limit_bytes=vmem_mib * MIB)


def _split_bf16(x):
    hi = x.astype(BF16)
    lo = (x - hi.astype(F32)).astype(BF16)
    return hi, lo


def _softplus(y):
    return jnp.maximum(y, 0.0) + jnp.log1p(jnp.exp(-jnp.abs(y)))


def _in_proj_kernel(x_ref, g_ref, w_ref, qg_ref, kg_ref, *rest, segments):
    out_refs = rest[:len(segments)]
    h_scr = rest[len(segments)]
    j = pl.program_id(1)

    @pl.when(j == 0)
    def _():
        x = x_ref[...]
        ms = jnp.mean(x * x, axis=-1, keepdims=True)
        h_scr[...] = (x * lax.rsqrt(ms + RMS_EPS) * g_ref[...]).astype(BF16)

    acc = jnp.dot(h_scr[...], w_ref[...].astype(BF16), preferred_element_type=F32)
    for (kind, start, stop), o_ref in zip(segments, out_refs):
        in_segment = pl.when((j >= start) & (j < stop)) if len(segments) > 1 else (lambda f: f())

        @in_segment
        def _(kind=kind, o_ref=o_ref):
            if kind == "sigmoid":
                o_ref[...] = jax.nn.sigmoid(acc).astype(o_ref.dtype)
            elif kind in ("qnorm", "knorm"):
                hg_ref = qg_ref if kind == "qnorm" else kg_ref
                for c in range(acc.shape[1] // ATT_HD):
                    sl = slice(c * ATT_HD, (c + 1) * ATT_HD)
                    y = acc[:, sl]
                    ms = jnp.mean(y * y, axis=-1, keepdims=True)
                    o_ref[:, sl] = y * lax.rsqrt(ms + RMS_EPS) * hg_ref[...]
            else:
                o_ref[...] = acc


def _in_proj(x, gain, w_all, layer, col0, q_gain, k_gain, seg_kinds, seg_widths, *, tm, tn):
    m, k = x.shape
    n = sum(seg_widths)
    assert col0 % tn == 0
    tile0 = col0 // tn
    segments, out_specs, out_shape = [], [], []
    start = 0
    for kind, width in zip(seg_kinds, seg_widths):
        tiles = width // tn
        assert tiles * tn == width
        segments.append((kind, start, start + tiles))
        out_specs.append(pl.BlockSpec((tm, tn), lambda i, j, s=start, t=tiles: (i, jnp.clip(j - s, 0, t - 1))))
        out_shape.append(jax.ShapeDtypeStruct((m, width), BF16 if kind == "sigmoid" else F32))
        start += tiles
    assert start * tn == n
    return pl.pallas_call(
        functools.partial(_in_proj_kernel, segments=tuple(segments)),
        grid=(m // tm, n // tn),
        in_specs=[
            pl.BlockSpec((tm, k), lambda i, j: (i, 0)),
            pl.BlockSpec((1, k), lambda i, j: (0, 0)),
            pl.BlockSpec((None, k, tn), lambda i, j: (layer, 0, tile0 + j)),
            pl.BlockSpec((1, ATT_HD), lambda i, j: (0, 0)),
            pl.BlockSpec((1, ATT_HD), lambda i, j: (0, 0)),
        ],
        out_specs=out_specs,
        out_shape=out_shape,
        scratch_shapes=[pltpu.VMEM((tm, k), BF16)],
        compiler_params=_cparams(("parallel", "arbitrary"), 48),
    )(x, gain.reshape(1, k), w_all, q_gain.reshape(1, ATT_HD), k_gain.reshape(1, ATT_HD))


def _matmul_res_kernel(a_ref, w_ref, r_ref, o_ref):
    o_ref[...] = r_ref[...] + jnp.dot(a_ref[...], w_ref[...], preferred_element_type=F32)


def _matmul_residual(a, w, res, *, tm):
    m, k = a.shape
    n = w.shape[1]
    return pl.pallas_call(
        _matmul_res_kernel,
        grid=(m // tm,),
        in_specs=[
            pl.BlockSpec((tm, k), lambda i: (i, 0)),
            pl.BlockSpec((k, n), lambda i: (0, 0), pipeline_mode=pl.Buffered(1)),
            pl.BlockSpec((tm, n), lambda i: (i, 0)),
        ],
        out_specs=pl.BlockSpec((tm, n), lambda i: (i, 0)),
        out_shape=jax.ShapeDtypeStruct((m, n), F32),
        compiler_params=_cparams(("parallel",), 40),
    )(a, w, res)


def _ffn_kernel(x_ref, g_ref, wg_ref, wu_ref, wd_ref, o_ref, h_scr, acc_scr):
    j = pl.program_id(1)

    @pl.when(j == 0)
    def _():
        x = x_ref[...]
        ms = jnp.mean(x * x, axis=-1, keepdims=True)
        h_scr[...] = (x * lax.rsqrt(ms + RMS_EPS) * g_ref[...]).astype(BF16)
        acc_scr[...] = jnp.zeros_like(acc_scr)

    h = h_scr[...]
    gate = jnp.dot(h, wg_ref[...], preferred_element_type=F32)
    up = jnp.dot(h, wu_ref[...], preferred_element_type=F32)
    act = (gate * jax.nn.sigmoid(gate) * up).astype(BF16)
    acc_scr[...] += jnp.dot(act, wd_ref[...], preferred_element_type=F32)

    @pl.when(j == pl.num_programs(1) - 1)
    def _():
        o_ref[...] = x_ref[...] + acc_scr[...]


def _ffn(x, gain, wg, wu, wd, *, tm, th):
    m, d = x.shape
    hidden = wg.shape[1]
    return pl.pallas_call(
        _ffn_kernel,
        grid=(m // tm, hidden // th),
        in_specs=[
            pl.BlockSpec((tm, d), lambda i, j: (i, 0)),
            pl.BlockSpec((1, d), lambda i, j: (0, 0)),
            pl.BlockSpec((d, th), lambda i, j: (0, j)),
            pl.BlockSpec((d, th), lambda i, j: (0, j)),
            pl.BlockSpec((th, d), lambda i, j: (j, 0)),
        ],
        out_specs=pl.BlockSpec((tm, d), lambda i, j: (i, 0)),
        out_shape=jax.ShapeDtypeStruct((m, d), F32),
        scratch_shapes=[pltpu.VMEM((tm, d), BF16), pltpu.VMEM((tm, d), F32)],
        compiler_params=_cparams(("parallel", "arbitrary"), 52),
    )(x, gain.reshape(1, d), wg, wu, wd)


def _gmlp_kernel(p_ref, lng_ref, lnb_ref, wm_ref, bs_ref, o_ref, *v_out, chunk):
    z = jax.nn.gelu(p_ref[...])
    width = z.shape[1] // 2
    u = z[:, :width]
    v = z[:, width:]
    vc = v - jnp.mean(v, axis=-1, keepdims=True)
    var = jnp.mean(vc * vc, axis=-1, keepdims=True)
    v = vc * lax.rsqrt(var + LN_EPS) * lng_ref[...] + lnb_ref[...]
    if v_out:
        v_out[0][...] = v
    gw = width // A_GROUPS
    for ci in range(z.shape[0] // chunk):
        rows = slice(ci * chunk, (ci + 1) * chunk)
        for g in range(A_GROUPS):
            cols = slice(g * gw, (g + 1) * gw)
            s = jnp.dot(wm_ref[g], v[rows, cols].astype(BF16), preferred_element_type=F32) + bs_ref[g]
            o_ref[rows, cols] = (u[rows, cols] * s).astype(o_ref.dtype)


def _gmlp(p_a, ln_g, ln_b, w_mix, b_mix, *, tm, chunk, want_v):
    m, w2 = p_a.shape
    width = w2 // 2
    out_shape = [jax.ShapeDtypeStruct((m, width), BF16)]
    out_specs = [pl.BlockSpec((tm, width), lambda i: (i, 0))]
    if want_v:
        out_shape.append(jax.ShapeDtypeStruct((m, width), F32))
        out_specs.append(pl.BlockSpec((tm, width), lambda i: (i, 0)))
    res = pl.pallas_call(
        functools.partial(_gmlp_kernel, chunk=chunk),
        grid=(m // tm,),
        in_specs=[
            pl.BlockSpec((tm, w2), lambda i: (i, 0)),
            pl.BlockSpec((1, width), lambda i: (0, 0)),
            pl.BlockSpec((1, width), lambda i: (0, 0)),
            pl.BlockSpec(w_mix.shape, lambda i: (0, 0, 0)),
            pl.BlockSpec(b_mix.shape, lambda i: (0, 0, 0)),
        ],
        out_specs=out_specs,
        out_shape=out_shape,
        compiler_params=_cparams(("parallel",), 32),
    )(p_a, ln_g.reshape(1, width), ln_b.reshape(1, width), w_mix, b_mix)
    return res if want_v else (res[0], None)


def _stick_terms(z):
    t = jnp.log(1.0 + jnp.exp(-jnp.abs(z)))
    return jnp.minimum(z, 0.0) - t, jnp.minimum(-z, 0.0) - t


def _suffix_sums(l1m, tri_ref):
    hi, lo = _split_bf16(l1m)
    n = l1m.shape[0]
    both = jnp.dot(jnp.concatenate([hi, lo], axis=0), tri_ref[...], preferred_element_type=F32)
    return both[:n] + both[n:]


def _sb_prompt_kernel(bias_ref, q_ref, k_ref, v_ref, tri_ref, o_ref, *, tq, tk, heads, scale):
    i = pl.program_id(1)
    band = tq // tk
    cols = [slice(h * ATT_HD, (h + 1) * ATT_HD) for h in range(heads)]
    q = [q_ref[0, :, c].astype(BF16) for c in cols]

    def visit(j, h, carry, acc, first_row=None):
        start = pl.multiple_of(j * tk, tk)
        k_t = k_ref[0, pl.ds(start, tk), cols[h]].astype(BF16)
        v_t = v_ref[0, pl.ds(start, tk), cols[h]].astype(BF16)
        r0 = first_row or 0
        s = lax.dot_general(q[h][r0:], k_t, (((1,), (1,)), ((), ())), preferred_element_type=F32)
        ls, l1m = _stick_terms(s * scale + bias_ref[h])
        if first_row is not None:
            row = lax.broadcasted_iota(jnp.int32, (tq - r0, tk), 0)
            col = lax.broadcasted_iota(jnp.int32, (tq - r0, tk), 1)
            vis = col < row
            l1m = jnp.where(vis, l1m, 0.0)
        att = jnp.exp(ls + _suffix_sums(l1m, tri_ref) + carry[r0:])
        if first_row is not None:
            att = jnp.where(vis, att, 0.0)
        carry_new = carry[r0:] + jnp.sum(l1m, axis=1, keepdims=True)
        acc_new = acc[r0:] + jnp.dot(att.astype(BF16), v_t, preferred_element_type=F32)
        if r0:
            carry_new = jnp.concatenate([carry[:r0], carry_new], axis=0)
            acc_new = jnp.concatenate([acc[:r0], acc_new], axis=0)
        return carry_new, acc_new

    state = [jnp.zeros((tq, 1), F32), jnp.zeros((tq, ATT_HD), F32)] * heads
    for t in reversed(range(band)):
        for h in range(heads):
            state[2 * h], state[2 * h + 1] = visit(i * band + t, h, state[2 * h], state[2 * h + 1],
                                                   first_row=t * tk)

    def body(jj, state):
        new = []
        for h in range(heads):
            new.extend(visit(i * band - 1 - jj, h, state[2 * h], state[2 * h + 1]))
        return tuple(new)

    state = lax.fori_loop(0, i * band, body, tuple(state))
    for h in range(heads):
        o_ref[0, :, cols[h]] = state[2 * h + 1].astype(o_ref.dtype)


def _strict_upper(n):
    r = lax.broadcasted_iota(jnp.int32, (n, n), 0)
    c = lax.broadcasted_iota(jnp.int32, (n, n), 1)
    return (r > c).astype(BF16)


def _sb_prompt(q, k, v, bias, *, tq, tk):
    b, l, w = q.shape
    heads = w // ATT_HD
    assert tq % tk == 0 and l % tq == 0
    return pl.pallas_call(
        functools.partial(_sb_prompt_kernel, tq=tq, tk=tk, heads=heads, scale=ATT_HD ** -0.5),
        grid=(b, l // tq),
        in_specs=[
            pl.BlockSpec(memory_space=pltpu.SMEM),
            pl.BlockSpec((1, tq, w), lambda bi, i: (bi, i, 0)),
            pl.BlockSpec((1, l, w), lambda bi, i: (bi, 0, 0), pipeline_mode=pl.Buffered(1)),
            pl.BlockSpec((1, l, w), lambda bi, i: (bi, 0, 0), pipeline_mode=pl.Buffered(1)),
            pl.BlockSpec((tk, tk), lambda bi, i: (0, 0)),
        ],
        out_specs=pl.BlockSpec((1, tq, w), lambda bi, i: (bi, i, 0)),
        out_shape=jax.ShapeDtypeStruct((b, l, w), BF16),
        compiler_params=_cparams(("parallel", "arbitrary"), 48),
    )(bias, q, k, v, _strict_upper(tk))


def _sb_sample_kernel(pt_ref, bias_ref, qrows_ref, knew_ref, vnew_ref, *rest, pages_per_step, heads, n_q, scale):
    k_refs = rest[:pages_per_step]
    v_refs = rest[pages_per_step:2 * pages_per_step]
    tri_ref, o_ref, carry_scr, acc_scr = rest[2 * pages_per_step:]
    s = pl.program_id(1)
    page = tri_ref.shape[0]
    n_rows = heads * n_q
    q = qrows_ref[0].astype(BF16)

    def block(k_blk, v_blk, masked):
        n_blk = k_blk.shape[0] // page
        sc = lax.dot_general(q, k_blk.astype(BF16), (((1,), (1,)), ((), ())), preferred_element_type=F32)
        z = sc * scale + bias_ref[:, 0:n_blk * page]
        ls, l1m = _stick_terms(z)
        if masked:
            row = lax.broadcasted_iota(jnp.int32, (n_rows, page), 0)
            col = lax.broadcasted_iota(jnp.int32, (n_rows, page), 1)
            vis = col < (row % n_q)
            l1m = jnp.where(vis, l1m, 0.0)
        pages = [slice(p * page, (p + 1) * page) for p in range(n_blk)]
        local = _suffix_sums(jnp.concatenate([l1m[:, sl] for sl in pages], axis=0), tri_ref)
        carry = carry_scr[...]
        between = []
        for p, sl in enumerate(pages):
            between.append(local[p * n_rows:(p + 1) * n_rows] + carry)
            carry = carry + jnp.sum(l1m[:, sl], axis=1, keepdims=True)
        att = jnp.exp(ls + jnp.concatenate(between, axis=1))
        if masked:
            att = jnp.where(vis, att, 0.0)
        acc_scr[...] += jnp.dot(att.astype(BF16), v_blk.astype(BF16), preferred_element_type=F32)
        carry_scr[...] = carry

    @pl.when(s == 0)
    def _():
        carry_scr[...] = jnp.zeros_like(carry_scr)
        acc_scr[...] = jnp.zeros_like(acc_scr)
        block(knew_ref[0], vnew_ref[0], True)

    def heads_to_lanes(ref):
        return jnp.concatenate([ref[pl.ds(h, page, stride=heads), :] for h in range(heads)], axis=1)

    block(jnp.concatenate([heads_to_lanes(r) for r in k_refs], axis=0),
          jnp.concatenate([heads_to_lanes(r) for r in v_refs], axis=0), False)

    @pl.when(s == pl.num_programs(1) - 1)
    def _():
        acc = acc_scr[...]
        for h in range(heads):
            o_ref[0, :, h * ATT_HD:(h + 1) * ATT_HD] = acc[h * n_q:(h + 1) * n_q, h * ATT_HD:(h + 1) * ATT_HD]


def _sb_sample(q, k_new, v_new, bias, cache_k, cache_v, page_table, layer, *, pages_per_step):
    b, n_q, w = q.shape
    heads = w // ATT_HD
    page = cache_k.shape[2] // heads
    n_pages = page_table.shape[1]
    n_rows = heads * n_q
    eye = jnp.eye(heads, dtype=F32)
    qrows = jnp.einsum("bthd,hg->bhtgd", q.reshape(b, n_q, heads, ATT_HD), eye).reshape(b, n_rows, w)
    pad = ((0, 0), (0, page - n_q), (0, 0))
    k_pad = jnp.pad(k_new, pad)
    v_pad = jnp.pad(v_new, pad)
    bias_rows = jnp.broadcast_to(jnp.repeat(bias, n_q)[:, None], (n_rows, page * pages_per_step))

    def page_spec(p):
        def imap(bi, s, pt):
            return (layer, pt[bi, n_pages - 1 - (s * pages_per_step + p)], 0, 0)
        return pl.BlockSpec((None, None, page * heads, ATT_HD), imap)

    grid_spec = pltpu.PrefetchScalarGridSpec(
        num_scalar_prefetch=1,
        grid=(b, n_pages // pages_per_step),
        in_specs=[
            pl.BlockSpec((n_rows, page * pages_per_step), lambda bi, s, pt: (0, 0)),
            pl.BlockSpec((1, n_rows, w), lambda bi, s, pt: (bi, 0, 0)),
            pl.BlockSpec((1, page, w), lambda bi, s, pt: (bi, 0, 0)),
            pl.BlockSpec((1, page, w), lambda bi, s, pt: (bi, 0, 0)),
        ] + [page_spec(p) for p in range(pages_per_step)] * 2 + [
            pl.BlockSpec((page, page), lambda bi, s, pt: (0, 0)),
        ],
        out_specs=pl.BlockSpec((1, n_q, w), lambda bi, s, pt: (bi, 0, 0)),
        scratch_shapes=[pltpu.VMEM((n_rows, 1), F32), pltpu.VMEM((n_rows, w), F32)],
    )
    return pl.pallas_call(
        functools.partial(_sb_sample_kernel, pages_per_step=pages_per_step, heads=heads, n_q=n_q,
                          scale=ATT_HD ** -0.5),
        grid_spec=grid_spec,
        out_shape=jax.ShapeDtypeStruct((b, n_q, w), F32),
        compiler_params=_cparams(("parallel", "arbitrary"), 40),
    )(page_table, bias_rows, qrows, k_pad, v_pad,
      *([cache_k] * pages_per_step), *([cache_v] * pages_per_step), _strict_upper(page))


def _head_sums(x, bd_ref):
    hi, lo = _split_bf16(x)
    return _bd_dot(hi, bd_ref) + _bd_dot(lo, bd_ref)


def _bd_dot(lhs, bd_ref):
    bd = bd_ref[...]
    outs = [jnp.dot(lhs[:, c:c + MXU_DIM], bd, preferred_element_type=F32)
            for c in range(0, lhs.shape[1], MXU_DIM)]
    return jnp.concatenate(outs, axis=1)


def _rwkv_kernel(p_ref, shift0_ref, s0_ref, mu_ref, w0_ref, a0_ref, lora_ref, gup_ref, kk_ref, ka_ref, rk_ref,
                 gng_ref, gnb_ref, bd_ref, eye_ref, eye2_ref,
                 o_ref, sout_ref,
                 st_scr, carry_scr, xs_scr, kn_scr, r_scr, w_scr, k_scr, v_scr, b_scr, g_scr, bonus_scr, oo_scr,
                 *, nb, tc, width):
    ci = pl.program_id(0)
    rows_per_b = RW_HD
    group_rows = 4 * SUBLANES if tc % (4 * SUBLANES) == 0 else SUBLANES
    steps = min(tc, group_rows)

    @pl.when(ci == 0)
    def _():
        st_scr[...] = s0_ref[...]
        carry_scr[...] = shift0_ref[...]
        xs_scr[...] = jnp.zeros_like(xs_scr)
        if tc % SUBLANES:
            v_scr[...] = jnp.zeros_like(v_scr)

    for b in range(nb):
        pc = p_ref[b]
        xs_scr[SUBLANES:SUBLANES + tc, :] = pc
        xs_scr[SUBLANES - 1:SUBLANES, :] = carry_scr[b:b + 1, :]
        prev = xs_scr[SUBLANES - 1:SUBLANES - 1 + tc, :]
        carry_scr[b:b + 1, :] = pc[tc - 1:tc, :]
        xs = pc + (prev - pc) * mu_ref[...]
        r = xs[:, 0:width]
        k = xs[:, width:2 * width]
        v = xs[:, 2 * width:3 * width]
        xwa = xs[:, 3 * width:3 * width + RW_LORA_W + RW_LORA_A]
        xg = xs[:, 3 * width + RW_LORA_W + RW_LORA_A:]
        lane = lax.broadcasted_iota(jnp.int32, xwa.shape, 1)
        lora_in = jnp.where(lane < RW_LORA_W, jnp.tanh(xwa), xwa).astype(BF16)
        lora = jnp.dot(lora_in, lora_ref[...], preferred_element_type=F32)
        w_log = -_softplus(-(w0_ref[...] + lora[:, :width])) - 0.5
        decay = jnp.exp(-jnp.exp(w_log))
        a = jax.nn.sigmoid(a0_ref[...] + lora[:, width:])
        g = jnp.dot(jax.nn.sigmoid(xg).astype(BF16), gup_ref[...], preferred_element_type=F32)
        kk = k * kk_ref[...]
        kk = kk * lax.rsqrt(jnp.maximum(_head_sums(kk * kk, bd_ref), KK_EPS))
        k2 = k * (1.0 + (a - 1.0) * ka_ref[...])
        kb = kk * a
        r_scr[b, 0:tc, :] = r
        w_scr[b, 0:tc, :] = decay
        k_scr[b, 0:tc, :] = k2
        v_scr[b, 0:tc, :] = v
        kn_scr[b, 0:tc, :] = kk
        b_scr[b, 0:tc, :] = kb
        g_scr[b, 0:tc, :] = g
        bonus_scr[b, 0:tc, :] = _head_sums(r * k2 * rk_ref[...], bd_ref) * v

    eye = eye_ref[...]
    eye_hi = eye.astype(BF16)
    eye_lo = eye2_ref[...]

    def bcast(row):
        return jnp.broadcast_to(row, (rows_per_b, width))

    def group(gi, _):
        base = pl.multiple_of(gi * group_rows, SUBLANES)
        blk = {name: [scr[b, pl.ds(base, steps), :] for b in range(nb)]
               for name, scr in (("r", r_scr), ("w", w_scr), ("k", k_scr), ("kn", kn_scr), ("b", b_scr))}
        state = [st_scr[b * rows_per_b:(b + 1) * rows_per_b, :] for b in range(nb)]
        nr = rows_per_b
        v_hi, v_lo = [], []
        for b in range(nb):
            v8 = v_scr[b, pl.ds(base, max(steps, SUBLANES)), :]
            hi = v8.astype(BF16).astype(F32)
            lo = v8 - hi
            lane = lax.broadcasted_iota(jnp.int32, lo.shape, 1)
            v_hi.append(hi)
            v_lo.append(jnp.where(lane % RW_HD < RW_HD // 2, pltpu.roll(lo, width - RW_HD // 2, 1),
                                  pltpu.roll(lo, RW_HD // 2, 1)))
        def seg_dot(parts):
            halves = [p[:, c:c + MXU_DIM] for p in parts for c in range(0, width, MXU_DIM)]
            res = jnp.dot(jnp.concatenate(halves, axis=0), bd_ref[...], preferred_element_type=F32)
            nh = width // MXU_DIM
            return [jnp.concatenate([res[(idx * nh + c) * nr:(idx * nh + c + 1) * nr] for c in range(nh)], axis=1)
                    for idx in range(len(parts))]

        def v_column(b, j):
            return (eye_hi * bcast(v_hi[b][j:j + 1, :].astype(BF16))
                    + eye_lo * bcast(v_lo[b][j:j + 1, :].astype(BF16)))

        out_rows = [[] for _ in range(nb)]

        def emit(b, o_col):
            out_rows[b].append(jnp.sum(o_col * eye, axis=0, keepdims=True))

        pending = [[] for _ in range(nb)]
        for j in range(steps):
            row = lambda name, b: bcast(blk[name][b][j:j + 1, :])
            for b in range(nb):
                cols = seg_dot([v_column(b, j)] + pending[b])
                for o_col in cols[1:]:
                    emit(b, o_col)
                (s_kk,) = seg_dot([(state[b] * row("kn", b)).astype(BF16)])
                state[b] = state[b] * row("w", b) - s_kk * row("b", b) + cols[0] * row("k", b)
                pending[b] = [(state[b] * row("r", b)).astype(BF16)]
        for b in range(nb):
            emit(b, seg_dot(pending[b])[0])
        for b in range(nb):
            st_scr[b * rows_per_b:(b + 1) * rows_per_b, :] = state[b]
            oo_scr[b, pl.ds(base, steps), :] = jnp.concatenate(out_rows[b], axis=0)
        return 0

    lax.fori_loop(0, tc // steps, group, 0)

    inv_n = 1.0 / RW_HD
    for b in range(nb):
        o = oo_scr[b, 0:tc, :]
        oc = o - _head_sums(o, bd_ref) * inv_n
        var = _head_sums(oc * oc, bd_ref) * inv_n
        on = oc * lax.rsqrt(var + GN_EPS) * gng_ref[...] + gnb_ref[...] + bonus_scr[b, 0:tc, :]
        o_ref[b] = (on * g_scr[b, 0:tc, :]).astype(o_ref.dtype)

    @pl.when(ci == pl.num_programs(0) - 1)
    def _():
        sout_ref[...] = st_scr[...]


def _rwkv(p_c, shift0, wkv0, mu, w0, w_up, a0, a_up, g_up, k_k, k_a, r_k, gn_g, gn_b, *, tc):
    nb, l, cs = p_c.shape
    width = w0.shape[0]
    heads = width // RW_HD
    s0 = jnp.transpose(wkv0.astype(F32), (0, 2, 1, 3)).reshape(nb * RW_HD, width)
    lora = jnp.zeros((RW_LORA_W + RW_LORA_A, 2 * width), F32)
    lora = lora.at[:RW_LORA_W, :width].set(w_up).at[RW_LORA_W:, width:].set(a_up).astype(BF16)
    lane = jnp.arange(MXU_DIM)
    bd = (lane[:, None] // RW_HD == lane[None, :] // RW_HD).astype(BF16)
    lane_k = jnp.arange(width)[None, :] % RW_HD
    eye = (jnp.arange(RW_HD)[:, None] == lane_k).astype(F32)
    eye2 = ((jnp.arange(RW_HD)[:, None] + RW_HD // 2) % RW_HD == lane_k).astype(BF16)
    row = lambda t: t.reshape(1, -1).astype(F32)
    full = lambda arr: pl.BlockSpec(arr.shape, lambda c: (0,) * arr.ndim)
    small = [shift0.astype(F32), s0, row(mu), row(w0), row(a0), lora, g_up.astype(BF16), row(k_k), row(k_a),
             row(r_k), row(gn_g), row(gn_b), bd, eye, eye2]
    tc_pad = -(-tc // SUBLANES) * SUBLANES
    chunk_scr = pltpu.VMEM((nb, tc_pad, width), F32)
    out, s_out = pl.pallas_call(
        functools.partial(_rwkv_kernel, nb=nb, tc=tc, width=width),
        grid=(l // tc,),
        in_specs=[pl.BlockSpec((nb, tc, cs), lambda c: (0, c, 0))] + [full(t) for t in small],
        out_specs=[pl.BlockSpec((nb, tc, width), lambda c: (0, c, 0)),
                   pl.BlockSpec((nb * RW_HD, width), lambda c: (0, 0))],
        out_shape=[jax.ShapeDtypeStruct((nb, l, width), BF16),
                   jax.ShapeDtypeStruct((nb * RW_HD, width), F32)],
        scratch_shapes=[pltpu.VMEM((nb * RW_HD, width), F32), pltpu.VMEM((nb, cs), F32),
                        pltpu.VMEM((SUBLANES + tc, cs), F32)] + [chunk_scr] * 9,
        compiler_params=_cparams(("arbitrary",), 40),
    )(p_c, *small)
    s_new = jnp.transpose(s_out.reshape(nb, RW_HD, heads, RW_HD), (0, 2, 1, 3))
    return out, s_new


def _conv_kernel(p_ref, c0_ref, w_ref, cb_ref, lng_ref, lnb_ref, o_ref, c1_ref, zc_scr, win_scr, *, tm, taps):
    i = pl.program_id(1)
    width = o_ref.shape[2]
    first = CONV_HALO - (taps - 1)

    @pl.when(i == 0)
    def _():
        zc_scr[0:first, :] = jnp.zeros((first, width), F32)
        zc_scr[first:CONV_HALO, :] = c0_ref[0]

    @pl.when(i > 0)
    def _():
        zc_scr[0:CONV_HALO, :] = zc_scr[tm:tm + CONV_HALO, :]

    p = p_ref[0]
    zc_scr[CONV_HALO:CONV_HALO + tm, :] = p[:, :width] * jax.nn.sigmoid(p[:, width:])
    y = jnp.zeros((tm, width), F32)
    for shift in range(SUBLANES):
        offsets = [o for o in range(first, first + taps) if o % SUBLANES == shift]
        if not offsets:
            continue
        rows = max(offsets) - shift + tm
        win_scr[0:rows, :] = zc_scr[shift:shift + rows, :]
        for o in offsets:
            y = y + win_scr[o - shift:o - shift + tm, :] * w_ref[o - first:o - first + 1, :]
    y = y + cb_ref[...]
    yc = y - jnp.mean(y, axis=-1, keepdims=True)
    var = jnp.mean(yc * yc, axis=-1, keepdims=True)
    yn = yc * lax.rsqrt(var + LN_EPS) * lng_ref[...] + lnb_ref[...]
    o_ref[0] = (yn * jax.nn.sigmoid(yn)).astype(o_ref.dtype)

    @pl.when(i == pl.num_programs(1) - 1)
    def _():
        c1_ref[0] = zc_scr[tm + first:tm + CONV_HALO, :]


def _conv(p_d, conv0, conv_w, conv_b, ln_g, ln_b, *, tm):
    nb, l, w2 = p_d.shape
    width = w2 // 2
    taps = conv_w.shape[0]
    row = lambda t: t.reshape(1, width).astype(F32)
    return pl.pallas_call(
        functools.partial(_conv_kernel, tm=tm, taps=taps),
        grid=(nb, l // tm),
        in_specs=[
            pl.BlockSpec((1, tm, w2), lambda b, i: (b, i, 0)),
            pl.BlockSpec((1, taps - 1, width), lambda b, i: (b, 0, 0)),
            pl.BlockSpec((taps, width), lambda b, i: (0, 0)),
            pl.BlockSpec((1, width), lambda b, i: (0, 0)),
            pl.BlockSpec((1, width), lambda b, i: (0, 0)),
            pl.BlockSpec((1, width), lambda b, i: (0, 0)),
        ],
        out_specs=[pl.BlockSpec((1, tm, width), lambda b, i: (b, i, 0)),
                   pl.BlockSpec((1, taps - 1, width), lambda b, i: (b, 0, 0))],
        out_shape=[jax.ShapeDtypeStruct((nb, l, width), BF16),
                   jax.ShapeDtypeStruct((nb, taps - 1, width), F32)],
        scratch_shapes=[pltpu.VMEM((CONV_HALO + tm, width), F32)] * 2,
        compiler_params=_cparams(("parallel", "arbitrary"), 32),
    )(p_d, conv0.astype(F32), conv_w.astype(F32), row(conv_b), row(ln_g), row(ln_b))


def _merge_kernel(pa_ref, pb_ref, pc_ref, pd_ref, wa_ref, wb_ref, wc_ref, wd_ref,
                  ga_ref, gb_ref, gc_ref, gd_ref, o_ref):
    acc = None
    for p_ref, w_ref, g_ref in ((pa_ref, wa_ref, ga_ref), (pb_ref, wb_ref, gb_ref),
                                (pc_ref, wc_ref, gc_ref), (pd_ref, wd_ref, gd_ref)):
        y = g_ref[...].astype(F32) * jnp.dot(p_ref[...], w_ref[...], preferred_element_type=F32)
        acc = y if acc is None else acc + y
    o_ref[...] = acc.astype(o_ref.dtype)


def _merge(pres, outs, gates, *, tm):
    m, width = pres[0].shape
    d = outs[0].shape[1]
    gate_spec = lambda bidx: pl.BlockSpec((tm, d), lambda i: (i, bidx))
    return pl.pallas_call(
        _merge_kernel,
        grid=(m // tm,),
        in_specs=[pl.BlockSpec((tm, width), lambda i: (i, 0))] * 4
                 + [pl.BlockSpec((width, d), lambda i: (0, 0), pipeline_mode=pl.Buffered(1))] * 4
                 + [gate_spec(bidx) for bidx in range(4)],
        out_specs=pl.BlockSpec((tm, d), lambda i: (i, 0)),
        out_shape=jax.ShapeDtypeStruct((m, d), BF16),
        compiler_params=_cparams(("parallel",), 48),
    )(*pres, *outs, gates, gates, gates, gates)


def _pick_tile(m, pref):
    return pref if m % pref == 0 else m


def _trunk_layer(x, lw, attend, shift0, wkv0, conv0, *, gmlp_chunk, want_v):
    nb, l, d = x.shape
    m = nb * l
    x2 = x.reshape(m, d)
    tm = _pick_tile(m, 1024)
    tm_s = _pick_tile(m, 512)
    w_in, layer = lw["w_in"]
    p_a, q, k, v, p_c, p_d = _in_proj(
        x2, lw["norm_mix"], w_in, layer, 0, lw["b_qn"], lw["b_kn"],
        ("plain", "qnorm", "knorm", "plain", "plain", "plain"), lw["in_widths"][:-1], tm=tm, tn=MXU_DIM)
    (gates,) = _in_proj(x2, lw["norm_mix"], lw["w_gate"], 0, 0, lw["b_qn"], lw["b_kn"],
                        ("sigmoid",), lw["in_widths"][-1:], tm=tm, tn=2 * MXU_DIM)

    width = q.shape[1]
    pre_a, v_a = _gmlp(p_a, lw["a_ln_g"], lw["a_ln_b"], lw["a_mix"], lw["a_bias"],
                       tm=_pick_tile(m, 256), chunk=gmlp_chunk, want_v=want_v)
    pre_b = attend(q.reshape(nb, l, width), k.reshape(nb, l, width), v.reshape(nb, l, width))
    pre_c, wkv1 = _rwkv(p_c.reshape(nb, l, -1), shift0, wkv0, lw["c_mu"], lw["c_w0"], lw["c_w_up"], lw["c_a0"],
                        lw["c_a_up"], lw["c_g_up"], lw["c_k_k"], lw["c_k_a"], lw["c_r_k"], lw["c_gn_g"],
                        lw["c_gn_b"], tc=min(l, 128))
    shift1 = p_c.reshape(nb, l, -1)[:, -1]
    pre_d, conv1 = _conv(p_d.reshape(nb, l, -1), conv0, lw["d_conv_w"], lw["d_conv_b"], lw["d_ln_g"], lw["d_ln_b"],
                         tm=min(l, 256))
    merged = _merge([pre_a, pre_b.reshape(m, width).astype(BF16), pre_c.reshape(m, width), pre_d.reshape(m, width)],
                    [lw["a_out"], lw["b_out"], lw["c_out"], lw["d_out"]], gates, tm=tm_s)
    x2 = _matmul_residual(merged, lw["w_mix_out"], x2, tm=tm_s)
    x2 = _ffn(x2, lw["norm_ffn"], lw["f_gate"], lw["f_up"], lw["f_down"], tm=tm_s, th=512)
    return x2.reshape(nb, l, d), k, v, wkv1, shift1, conv1, v_a


def kernel(x_prompt, x_sample, cache_k, cache_v, page_table, state_wkv, state_shift, state_conv, norm_mix, w_in, a_ln_g, a_ln_b, a_ws, a_bs, a_out, b_qn, b_kn, b_bias, b_out, c_mu, c_w0, c_w_up, c_a0, c_a_up, c_g_up, c_k_k, c_k_a, c_r_k, c_gn_g, c_gn_b, c_out, d_conv_w, d_conv_b, d_ln_g, d_ln_b, d_out, w_mix_out, norm_ffn, f_gate, f_up, f_down):
    depth = w_in.shape[0]
    bp, seq, d_model = x_prompt.shape
    bs, dec = x_sample.shape[:2]
    a_w = a_ln_g.shape[1]
    bw = b_out.shape[1]
    heads = b_bias.shape[1]
    c_shift = c_mu.shape[1]
    d_w = d_ln_g.shape[1]
    chunk = a_ws.shape[2]
    taps = d_conv_w.shape[1]
    gw = a_w // A_GROUPS
    n_phys, page = cache_k.shape[1:3]
    cache_k = cache_k.reshape(depth, n_phys, page * heads, ATT_HD)
    cache_v = cache_v.reshape(depth, n_phys, page * heads, ATT_HD)
    in_widths = (2 * a_w, bw, bw, bw, c_shift, 2 * d_w, 4 * d_model)
    causal = jnp.tril(jnp.ones((chunk, chunk), dtype=bool))
    causal_s = jnp.tril(jnp.ones((dec, dec), dtype=bool))

    yp, ys = x_prompt, x_sample
    outs = [[] for _ in range(11)]
    for l in range(depth):
        lw = {
            "norm_mix": norm_mix[l], "w_in": (w_in, l), "in_widths": in_widths,
            "w_gate": w_in[l][None, :, -in_widths[-1]:].astype(BF16),
            "a_ln_g": a_ln_g[l], "a_ln_b": a_ln_b[l], "a_out": a_out[l].astype(BF16),
            "b_qn": b_qn[l], "b_kn": b_kn[l], "b_out": b_out[l].astype(BF16),
            "c_mu": c_mu[l], "c_w0": c_w0[l], "c_w_up": c_w_up[l], "c_a0": c_a0[l], "c_a_up": c_a_up[l],
            "c_g_up": c_g_up[l], "c_k_k": c_k_k[l], "c_k_a": c_k_a[l], "c_r_k": c_r_k[l],
            "c_gn_g": c_gn_g[l], "c_gn_b": c_gn_b[l], "c_out": c_out[l].astype(BF16),
            "d_conv_w": d_conv_w[l], "d_conv_b": d_conv_b[l], "d_ln_g": d_ln_g[l], "d_ln_b": d_ln_b[l],
            "d_out": d_out[l].astype(BF16),
            "w_mix_out": w_mix_out[l].astype(BF16), "norm_ffn": norm_ffn[l],
            "f_gate": f_gate[l].astype(BF16), "f_up": f_up[l].astype(BF16), "f_down": f_down[l].astype(BF16),
        }
        lw_p = dict(lw)
        lw_p["a_mix"] = jnp.where(causal, a_ws[l], 0.0).astype(BF16)
        lw_p["a_bias"] = jnp.broadcast_to(a_bs[l][:, :, None], (A_GROUPS, chunk, gw)).astype(F32)
        attend_p = functools.partial(_sb_prompt, bias=b_bias[l], tq=1024, tk=256)
        yp, kp, vp, wp, sp, cp, _ = _trunk_layer(
            yp, lw_p, attend_p, jnp.zeros((bp, c_shift), F32), jnp.zeros((bp, c_w0.shape[1] // RW_HD, RW_HD, RW_HD), F32),
            jnp.zeros((bp, taps - 1, d_w), F32), gmlp_chunk=chunk, want_v=False)

        lw_s = dict(lw)
        w_small = jnp.where(causal_s, a_ws[l][:, :dec, :dec], 0.0)
        lw_s["a_mix"] = jnp.einsum("ab,gts->gatbs", jnp.eye(bs, dtype=F32), w_small).reshape(
            A_GROUPS, bs * dec, bs * dec).astype(BF16)
        lw_s["a_bias"] = jnp.broadcast_to(jnp.tile(a_bs[l][:, :dec], (1, bs))[:, :, None],
                                          (A_GROUPS, bs * dec, gw)).astype(F32)
        attend_s = lambda q, k, v: _sb_sample(q, k, v, b_bias[l], cache_k, cache_v, page_table, l, pages_per_step=16)
        ys, ksn, vsn, wsn, ssn, csn, gvs = _trunk_layer(
            ys, lw_s, attend_s, state_shift[l], state_wkv[l], state_conv[l], gmlp_chunk=bs * dec, want_v=True)

        layer_out = (kp.reshape(bp, seq, heads, ATT_HD), vp.reshape(bp, seq, heads, ATT_HD),
                     ksn.reshape(bs, dec, heads, ATT_HD), vsn.reshape(bs, dec, heads, ATT_HD),
                     wp, wsn, sp, ssn, cp, csn, gvs.reshape(bs, dec, a_w))
        for acc, val in zip(outs, layer_out):
            acc.append(val)
    return (yp, ys) + tuple(jnp.stack(o) for o in outs)
```
